```python
import jax, jax.numpy as jnp
from jax import lax
import numpy as np

D_MODEL = 1024
BATCH = 4
SEQ = 8192
DEPTH = 4

EPS = 1e-6
D_FF = 2816
MLA_HEADS = 8
MLA_Q_LORA = 256
MLA_KV_LORA = 128
MLA_NOPE = 64
MLA_ROPE = 32
MLA_V = 64
MLA_QK = MLA_NOPE + MLA_ROPE
ROPE_BASE = 10000.0
ATTN_BLOCK = 128
GDN_HEADS = 8
GDN_DK = 64
GDN_DV = 64
GDN_CONV = 4
GDN_CHUNK = 64
GDN_QKV_W = GDN_HEADS * (2 * GDN_DK + GDN_DV)
SG_GROUPS = 8
SG_WIDTH = 2048
SG_CHUNK = 128
AB_WIDTHS = (MLA_Q_LORA, MLA_KV_LORA, MLA_ROPE, GDN_QKV_W, GDN_HEADS * GDN_DV, GDN_HEADS, GDN_HEADS)
IN_AB = sum(AB_WIDTHS)
MIX_AB = MLA_HEADS * MLA_V + GDN_HEADS * GDN_DV

kernel_name = "hybrid_mla_gdn_sgmlp_macaron"


def rms_norm(x, w):
    xf = x.astype(jnp.float32)
    y = xf * lax.rsqrt(jnp.mean(xf * xf, axis=-1, keepdims=True) + EPS)
    return (y * w.astype(jnp.float32)).astype(x.dtype)


def l2_norm(x):
    return x * lax.rsqrt(jnp.sum(x * x, axis=-1, keepdims=True) + EPS)


def swiglu(h, w_gate, w_up, w_down):
    return (jax.nn.silu(h @ w_gate) * (h @ w_up)) @ w_down


def rope_tables(positions):
    inv_freq = ROPE_BASE ** (-jnp.arange(0, MLA_ROPE, 2, dtype=jnp.float32) / MLA_ROPE)
    ang = positions.astype(jnp.float32)[..., None] * inv_freq
    return jnp.cos(ang)[:, :, None, :], jnp.sin(ang)[:, :, None, :]


def apply_rope(t, cos, sin):
    half = MLA_ROPE // 2
    t_nope, r1, r2 = t[..., :MLA_NOPE], t[..., MLA_NOPE:MLA_NOPE + half], t[..., MLA_NOPE + half:]
    c, s = cos.astype(t.dtype), sin.astype(t.dtype)
    return jnp.concatenate([t_nope, r1 * c - r2 * s, r2 * c + r1 * s], axis=-1)


def causal_block_attention(q, k, v):
    b, s, h, dq = q.shape
    dv = v.shape[-1]
    nb = s // ATTN_BLOCK
    scale = dq ** -0.5
    qb = q.astype(jnp.float32).reshape(b, nb, ATTN_BLOCK, h, dq).transpose(1, 0, 3, 2, 4)
    kh = k.astype(jnp.float32).transpose(0, 2, 1, 3)
    vh = v.astype(jnp.float32).transpose(0, 2, 1, 3)
    key_idx = jnp.arange(s)

    def one_block(args):
        q_blk, blk = args
        sc = jnp.einsum('bhqd,bhkd->bhqk', q_blk, kh) * scale
        q_idx = blk * ATTN_BLOCK + jnp.arange(ATTN_BLOCK)
        sc = jnp.where(key_idx[None, :] <= q_idx[:, None], sc, -jnp.inf)
        p = jax.nn.softmax(sc, axis=-1)
        return jnp.einsum('bhqk,bhkd->bhqd', p, vh)

    o = lax.map(one_block, (qb, jnp.arange(nb)))
    return o.transpose(1, 0, 3, 2, 4).reshape(b, s, h * dv).astype(v.dtype)


def causal_short_conv(x, w):
    kw, s = w.shape[0], x.shape[1]
    xp = jnp.pad(x, ((0, 0), (kw - 1, 0), (0, 0)))
    return sum(xp[:, j:j + s] * w[j] for j in range(kw))


def gated_delta_rule(q, k, v, beta, g):
    b, s, h, dk = q.shape
    dv = v.shape[-1]
    c = GDN_CHUNK
    n = s // c

    def chunks(t):
        return t.reshape(b, n, c, h, -1).transpose(0, 3, 1, 2, 4)

    q, k, v = chunks(q), chunks(k), chunks(v)
    beta = beta.reshape(b, n, c, h).transpose(0, 3, 1, 2)
    gc = jnp.cumsum(g.reshape(b, n, c, h).transpose(0, 3, 1, 2), axis=-1)
    tril = jnp.tril(jnp.ones((c, c), bool))
    strict = jnp.tril(jnp.ones((c, c), bool), -1)
    decay = jnp.exp(jnp.where(tril, gc[..., :, None] - gc[..., None, :], -jnp.inf))
    kb = k * beta[..., None]
    vb = v * beta[..., None]
    a = jnp.where(strict, jnp.einsum('bhnid,bhnjd->bhnij', kb, k) * decay, 0.0) + jnp.eye(c, dtype=jnp.float32)
    rhs = jnp.concatenate([vb, kb * jnp.exp(gc)[..., None]], axis=-1)
    sol = lax.linalg.triangular_solve(a, rhs, left_side=True, lower=True, unit_diagonal=True)
    u, w = sol[..., :dv], sol[..., dv:]
    attn = jnp.einsum('bhnid,bhnjd->bhnij', q, k) * decay
    q_dec = q * jnp.exp(gc)[..., None]
    g_last = gc[..., -1:]
    k_tail = k * jnp.exp(g_last - gc)[..., None]

    def step(state, xs):
        u_i, w_i, a_i, qd_i, kt_i, gl_i = xs
        v_new = u_i - jnp.einsum('bhcd,bhde->bhce', w_i, state)
        o_i = jnp.einsum('bhcd,bhde->bhce', qd_i, state) + jnp.einsum('bhcj,bhje->bhce', a_i, v_new)
        state = state * jnp.exp(gl_i)[..., None] + jnp.einsum('bhcd,bhce->bhde', kt_i, v_new)
        return state, o_i

    xs = tuple(jnp.moveaxis(t, 2, 0) for t in (u, w, attn, q_dec, k_tail, g_last))
    state0 = jnp.zeros((b, h, dk, dv), jnp.float32)
    _, o = lax.scan(step, state0, xs)
    return o.transpose(1, 0, 3, 2, 4).reshape(b, s, h, dv)


def mixer_mla_gdn(h, cos, sin, w_in, q_norm, w_q_b, kv_norm, w_kv_b, qk_norm_q, qk_norm_k,
                  conv_w, a_log, dt_bias, out_norm, w_out):
    b, s, _ = h.shape
    idx = [int(i) for i in np.cumsum(AB_WIDTHS)[:-1]]
    c_q, c_kv, k_r, qkv, z, b_logit, a_logit = jnp.split(h @ w_in, idx, axis=-1)
    q = (rms_norm(c_q, q_norm) @ w_q_b).reshape(b, s, MLA_HEADS, MLA_QK)
    kv = (rms_norm(c_kv, kv_norm) @ w_kv_b).reshape(b, s, MLA_HEADS, MLA_NOPE + MLA_V)
    k_nope, v = kv[..., :MLA_NOPE], kv[..., MLA_NOPE:]
    k = jnp.concatenate([k_nope, jnp.broadcast_to(k_r[:, :, None, :], (b, s, MLA_HEADS, MLA_ROPE))], axis=-1)
    q = apply_rope(rms_norm(q, qk_norm_q), cos, sin)
    k = apply_rope(rms_norm(k, qk_norm_k), cos, sin)
    y_mla = causal_block_attention(q, k, v)
    qkv = jax.nn.silu(causal_short_conv(qkv, conv_w)).astype(jnp.float32)
    gq, gk, gv = jnp.split(qkv, [GDN_HEADS * GDN_DK, 2 * GDN_HEADS * GDN_DK], axis=-1)
    gq = l2_norm(gq.reshape(b, s, GDN_HEADS, GDN_DK)) * (GDN_DK ** -0.5)
    gk = l2_norm(gk.reshape(b, s, GDN_HEADS, GDN_DK))
    gv = gv.reshape(b, s, GDN_HEADS, GDN_DV)
    beta = jax.nn.sigmoid(b_logit.astype(jnp.float32))
    g = -jnp.exp(a_log.astype(jnp.float32)) * jax.nn.softplus(a_logit.astype(jnp.float32) + dt_bias.astype(jnp.float32))
    o = gated_delta_rule(gq, gk, gv, beta, g).astype(h.dtype)
    o = rms_norm(o, out_norm) * jax.nn.silu(z.reshape(b, s, GDN_HEADS, GDN_DV))
    y = jnp.concatenate([y_mla, o.reshape(b, s, GDN_HEADS * GDN_DV)], axis=-1)
    return y @ w_out


def mixer_spatial_gating(h, w_in, v_norm, w_s, b_s, w_out):
    b, s, _ = h.shape
    n = s // SG_CHUNK
    cg = SG_WIDTH // SG_GROUPS
    u, v = jnp.split(jax.nn.gelu(h @ w_in, approximate=False), 2, axis=-1)
    v = rms_norm(v.reshape(b, n, SG_CHUNK, SG_GROUPS, cg), v_norm.reshape(SG_GROUPS, cg))
    w_causal = jnp.where(jnp.tril(jnp.ones((SG_CHUNK, SG_CHUNK), bool)), w_s, 0)
    gate = jnp.einsum('gij,bnjgc->bnigc', w_causal, v) + b_s.T[:, :, None]
    return (u * gate.reshape(b, s, SG_WIDTH)) @ w_out


def setup_inputs(seed: int = 0) -> dict:
    key = jax.random.key(seed)
    ks = jax.random.split(key, 24)
    ne = (DEPTH + 1) // 2
    no = DEPTH // 2
    f32 = jnp.float32

    def nrm(k, shape, fan_in):
        return jax.random.normal(k, shape, f32) * (fan_in ** -0.5)

    def gain(k, shape, sd=0.05):
        return 1.0 + sd * jax.random.normal(k, shape, f32)

    dt = jnp.exp(jax.random.uniform(ks[15], (ne, GDN_HEADS), f32, np.log(0.001), np.log(0.1)))
    return {
        "x": jax.random.normal(ks[0], (BATCH, SEQ, D_MODEL), f32),
        "positions": (jnp.arange(SEQ, dtype=jnp.int32)[None, :]
                      + jax.random.randint(ks[1], (BATCH, 1), 0, 1024, jnp.int32)),
        "norm_w": gain(ks[2], (DEPTH, 3, D_MODEL)),
        "ffn_w_gate": nrm(ks[3], (DEPTH, 2, D_MODEL, D_FF), D_MODEL),
        "ffn_w_up": nrm(ks[4], (DEPTH, 2, D_MODEL, D_FF), D_MODEL),
        "ffn_w_down": nrm(ks[5], (DEPTH, 2, D_FF, D_MODEL), D_FF),
        "ab_w_in": nrm(ks[6], (ne, D_MODEL, IN_AB), D_MODEL),
        "mla_q_norm": gain(ks[7], (ne, MLA_Q_LORA)),
        "mla_w_q_b": nrm(ks[8], (ne, MLA_Q_LORA, MLA_HEADS * MLA_QK), MLA_Q_LORA),
        "mla_kv_norm": gain(ks[9], (ne, MLA_KV_LORA)),
        "mla_w_kv_b": nrm(ks[10], (ne, MLA_KV_LORA, MLA_HEADS * (MLA_NOPE + MLA_V)), MLA_KV_LORA),
        "mla_qk_norm_q": gain(ks[11], (ne, MLA_QK)),
        "mla_qk_norm_k": gain(ks[12], (ne, MLA_QK)),
        "gdn_conv_w": nrm(ks[13], (ne, GDN_CONV, GDN_QKV_W), GDN_CONV),
        "gdn_a_log": jnp.log(jax.random.uniform(ks[14], (ne, GDN_HEADS), f32, 1.0, 16.0)),
        "gdn_dt_bias": jnp.log(jnp.expm1(dt)),
        "gdn_out_norm": gain(ks[16], (ne, GDN_DV)),
        "ab_w_out": nrm(ks[17], (ne, MIX_AB, D_MODEL), MIX_AB),
        "sg_w_in": nrm(ks[18], (no, D_MODEL, 2 * SG_WIDTH), D_MODEL),
        "sg_v_norm": gain(ks[19], (no, SG_WIDTH)),
        "sg_w_s": nrm(ks[20], (no, SG_GROUPS, SG_CHUNK, SG_CHUNK), SG_CHUNK),
        "sg_b_s": gain(ks[21], (no, SG_GROUPS, SG_CHUNK), 0.1),
        "sg_w_out": nrm(ks[22], (no, SG_WIDTH, D_MODEL), SG_WIDTH),
    }


def reference(x, positions, norm_w, ffn_w_gate, ffn_w_up, ffn_w_down, ab_w_in, mla_q_norm, mla_w_q_b,
              mla_kv_norm, mla_w_kv_b, mla_qk_norm_q, mla_qk_norm_k, gdn_conv_w, gdn_a_log, gdn_dt_bias,
              gdn_out_norm, ab_w_out, sg_w_in, sg_v_norm, sg_w_s, sg_b_s, sg_w_out):
    cos, sin = rope_tables(positions)
    for l in range(DEPTH):
        i = l // 2
        x = x + 0.5 * swiglu(rms_norm(x, norm_w[l, 0]), ffn_w_gate[l, 0], ffn_w_up[l, 0], ffn_w_down[l, 0])
        h = rms_norm(x, norm_w[l, 1])
        if l % 2 == 0:
            x = x + mixer_mla_gdn(h, cos, sin, ab_w_in[i], mla_q_norm[i], mla_w_q_b[i], mla_kv_norm[i],
                                  mla_w_kv_b[i], mla_qk_norm_q[i], mla_qk_norm_k[i], gdn_conv_w[i],
                                  gdn_a_log[i], gdn_dt_bias[i], gdn_out_norm[i], ab_w_out[i])
        else:
            x = x + mixer_spatial_gating(h, sg_w_in[i], sg_v_norm[i], sg_w_s[i], sg_b_s[i], sg_w_out[i])
        x = x + 0.5 * swiglu(rms_norm(x, norm_w[l, 2]), ffn_w_gate[l, 1], ffn_w_up[l, 1], ffn_w_down[l, 1])
    return x
```

```python
import functools

import numpy as np
import jax
import jax.numpy as jnp
from jax import lax
from jax.experimental import pallas as pl
from jax.experimental.pallas import tpu as pltpu

F32 = jnp.float32
BF16 = jnp.bfloat16

D_MODEL = 1024
D_FF = 2816
EPS = 1e-6
N_HEADS = 8
MLA_Q_LORA = 256
MLA_KV_LORA = 128
MLA_NOPE = 64
MLA_ROPE = 32
MLA_V = 64
MLA_QK = MLA_NOPE + MLA_ROPE
ROPE_BASE = 10000.0
GDN_DK = 64
GDN_DV = 64
GDN_CONV = 4
GDN_CHUNK = 64
GDN_W = N_HEADS * GDN_DK
SG_GROUPS = 8
SG_WIDTH = 2048
SG_CHUNK = 128
SG_CG = SG_WIDTH // SG_GROUPS

LANES = 128
VMEM_LIMIT_BYTES = 56 * 1024 * 1024
HEAD_PAD = LANES

_C_Q0 = 0
_C_KV0 = _C_Q0 + MLA_Q_LORA
_C_KR0 = _C_KV0 + MLA_KV_LORA
_C_KRP0 = _C_KR0 + LANES
_C_QKV0 = _C_KRP0 + LANES
_C_Z0 = _C_QKV0 + 3 * GDN_W
_C_B0 = _C_Z0 + GDN_W
_C_A0 = _C_B0 + GDN_W
IN_AB_PAD = _C_A0 + GDN_W

_Q_FOLD = (MLA_QK ** -0.5) * float(np.log2(np.e))


def _cparams(sem):
    return pltpu.CompilerParams(dimension_semantics=sem, vmem_limit_bytes=VMEM_LIMIT_BYTES)


def _rms(x, w):
    return x * lax.rsqrt(jnp.mean(x * x, axis=-1, keepdims=True) + EPS) * w


def _const_spec(shape):
    nd = len(shape)
    return pl.BlockSpec(shape, lambda *_: (0,) * nd, pipeline_mode=pl.Buffered(1))


def _ffn_body(x_ref, nw_ref, wg_ref, wu_ref, wd_ref, o_ref, *, fc):
    x = x_ref[...]
    h = _rms(x, nw_ref[...]).astype(BF16)
    acc = None
    for c in range(D_FF // fc):
        sl = slice(c * fc, (c + 1) * fc)
        g = jnp.dot(h, wg_ref[:, sl], preferred_element_type=F32)
        u = jnp.dot(h, wu_ref[:, sl], preferred_element_type=F32)
        a = (g * jax.nn.sigmoid(g) * u).astype(BF16)
        d = jnp.dot(a, wd_ref[sl, :], preferred_element_type=F32)
        acc = d if acc is None else acc + d
    o_ref[...] = x + 0.5 * acc


def _ffn(x, nw, wg, wu, wd, *, tm=512, fc=1408):
    t = x.shape[0]
    return pl.pallas_call(
        functools.partial(_ffn_body, fc=fc),
        out_shape=jax.ShapeDtypeStruct((t, D_MODEL), F32),
        grid=(t // tm,),
        in_specs=[
            pl.BlockSpec((tm, D_MODEL), lambda i: (i, 0)),
            _const_spec((1, D_MODEL)),
            _const_spec((D_MODEL, D_FF)),
            _const_spec((D_MODEL, D_FF)),
            _const_spec((D_FF, D_MODEL)),
        ],
        out_specs=pl.BlockSpec((tm, D_MODEL), lambda i: (i, 0)),
        compiler_params=_cparams(("arbitrary",)),
        name="ffn",
    )(x, nw, wg, wu, wd)


def _rope_body(pos_ref, invf_ref, cos_ref, sin_ref):
    ang = pos_ref[...] * invf_ref[...]
    cos_ref[...] = jnp.cos(ang)
    sin_ref[...] = jnp.sin(ang)


def _rope_tables(positions):
    b, s = positions.shape
    t = b * s
    half = MLA_ROPE // 2
    inv_freq = ROPE_BASE ** (-jnp.arange(0, MLA_ROPE, 2, dtype=F32) / MLA_ROPE)
    per_row = LANES // half
    pos = jnp.repeat(positions.reshape(t).astype(F32), half).reshape(t // per_row, LANES)
    invf = jnp.tile(inv_freq, per_row).reshape(1, LANES)
    rows = t // per_row
    tr = min(rows, 512)
    cos, sin = pl.pallas_call(
        _rope_body,
        out_shape=(jax.ShapeDtypeStruct((rows, LANES), F32),) * 2,
        grid=(rows // tr,),
        in_specs=[pl.BlockSpec((tr, LANES), lambda i: (i, 0)), _const_spec((1, LANES))],
        out_specs=(pl.BlockSpec((tr, LANES), lambda i: (i, 0)),) * 2,
        compiler_params=_cparams(("arbitrary",)),
        name="rope_tables",
    )(pos, invf)
    cos = cos.reshape(t, half)
    sin = sin.reshape(t, half)
    pad = LANES - MLA_QK
    cos_full = jnp.concatenate([jnp.ones((t, MLA_NOPE), F32), cos, cos, jnp.ones((t, pad), F32)], axis=-1)
    sin_full = jnp.concatenate([jnp.zeros((t, MLA_NOPE), F32), sin, sin, jnp.zeros((t, pad), F32)], axis=-1)
    return cos_full, sin_full


def _softplus(x):
    return jnp.maximum(x, 0.0) + jnp.log1p(jnp.exp(-jnp.abs(x)))


def _ab_in_body(x_ref, cos_ref, sin_ref, nw_ref, w1_ref, qn_ref, wq_ref, kvn_ref, wkv_ref,
                nq_ref, nqp_ref, nk_ref, nkp_ref, convw_ref, alog_ref, dtb_ref,
                q_out, k_out, v_out, qkv_out, beta_out, g_out, z_out,
                carry_ref, *, tm, tiles_per_seq):
    i = pl.program_id(0)
    x = x_ref[...]
    hb = _rms(x, nw_ref[...]).astype(BF16)
    p = jnp.dot(hb, w1_ref[...], preferred_element_type=F32)

    c_q = p[:, _C_Q0:_C_KV0]
    c_kv = p[:, _C_KV0:_C_KR0]
    kr = p[:, _C_KR0:_C_KRP0]
    krp = p[:, _C_KRP0:_C_QKV0]
    cqn = _rms(c_q, qn_ref[...]).astype(BF16)
    qq = jnp.dot(cqn, wq_ref[...], preferred_element_type=F32)
    ckvn = _rms(c_kv, kvn_ref[...]).astype(BF16)
    kk = jnp.dot(ckvn, wkv_ref[...], preferred_element_type=F32)
    cos = cos_ref[...]
    sin = sin_ref[...]
    hw = N_HEADS * HEAD_PAD
    for h in range(N_HEADS):
        sl = slice(h * HEAD_PAD, (h + 1) * HEAD_PAD)
        qh = qq[:, sl]
        qph = qq[:, hw + h * HEAD_PAD: hw + (h + 1) * HEAD_PAD]
        rq = lax.rsqrt(jnp.sum(qh * qh, axis=-1, keepdims=True) * (1.0 / MLA_QK) + EPS)
        qr = (qh * nq_ref[...] * cos + qph * nqp_ref[...] * sin) * (rq * _Q_FOLD)
        q_out[:, sl] = qr.astype(BF16)
        kh = kk[:, sl] + kr
        rk = lax.rsqrt(jnp.sum(kh * kh, axis=-1, keepdims=True) * (1.0 / MLA_QK) + EPS)
        kro = (kh * nk_ref[...] * cos + krp * nkp_ref[...] * sin) * rk
        k_out[:, sl] = kro.astype(BF16)
    v_out[...] = kk[:, hw:].astype(BF16)

    qkv_pre = p[:, _C_QKV0:_C_Z0]

    @pl.when(i % tiles_per_seq == 0)
    def _():
        carry_ref[0:8, :] = jnp.zeros((8, 3 * GDN_W), F32)

    carry_ref[8:8 + tm, :] = qkv_pre
    conv = qkv_pre * convw_ref[GDN_CONV - 1:GDN_CONV, :]
    for j in range(GDN_CONV - 1):
        off = 8 - (GDN_CONV - 1) + j
        conv = conv + carry_ref[off:off + tm, :] * convw_ref[j:j + 1, :]
    carry_ref[0:8, :] = qkv_pre[tm - 8:tm, :]
    qkv_out[...] = conv * jax.nn.sigmoid(conv)
    beta_out[...] = jax.nn.sigmoid(p[:, _C_B0:_C_A0])
    g_out[...] = -jnp.exp(alog_ref[...]) * _softplus(p[:, _C_A0:IN_AB_PAD] + dtb_ref[...])
    z_out[...] = p[:, _C_Z0:_C_B0]


def _ab_in(x, cos_full, sin_full, wts, *, seq, tm=256):
    t = x.shape[0]
    tiles_per_seq = seq // tm
    row = lambda i: (i, 0)
    hw = N_HEADS * HEAD_PAD
    outs = (
        jax.ShapeDtypeStruct((t, hw), BF16),
        jax.ShapeDtypeStruct((t, hw), BF16),
        jax.ShapeDtypeStruct((t, N_HEADS * MLA_V), BF16),
        jax.ShapeDtypeStruct((t, 3 * GDN_W), F32),
        jax.ShapeDtypeStruct((t, GDN_W), F32),
        jax.ShapeDtypeStruct((t, GDN_W), F32),
        jax.ShapeDtypeStruct((t, GDN_W), F32),
    )
    consts = [wts["nw"], wts["w1"], wts["q_norm"], wts["wq"], wts["kv_norm"], wts["wkv"],
              wts["nq"], wts["nqp"], wts["nk"], wts["nkp"], wts["conv_w"], wts["a_log"], wts["dt_bias"]]
    return pl.pallas_call(
        functools.partial(_ab_in_body, tm=tm, tiles_per_seq=tiles_per_seq),
        out_shape=outs,
        grid=(t // tm,),
        in_specs=[pl.BlockSpec((tm, D_MODEL), row), pl.BlockSpec((tm, LANES), row),
                  pl.BlockSpec((tm, LANES), row)] + [_const_spec(c.shape) for c in consts],
        out_specs=tuple(pl.BlockSpec((tm, o.shape[1]), row) for o in outs),
        scratch_shapes=[pltpu.VMEM((tm + 8, 3 * GDN_W), F32)],
        compiler_params=_cparams(("arbitrary",)),
        name="ab_in",
    )(x, cos_full, sin_full, *consts)


def _attn_body(q_ref, k_ref, v_ref, o_ref, m_ref, l_ref, acc_ref, *, tq):
    qi = pl.program_id(2)
    nt = (((1,), (1,)), ((), ()))
    m_ref[...] = jnp.full(m_ref.shape, -jnp.inf, F32)
    l_ref[...] = jnp.zeros(l_ref.shape, F32)
    acc_ref[...] = jnp.zeros(acc_ref.shape, F32)

    def block(kb, masked):
        row0 = pl.multiple_of(kb * tq, tq)
        v = v_ref[pl.ds(row0, tq), :]
        for e in range(2):
            q = q_ref[:, e * HEAD_PAD:(e + 1) * HEAD_PAD]
            k = k_ref[pl.ds(row0, tq), e * HEAD_PAD:(e + 1) * HEAD_PAD]
            s = lax.dot_general(q, k, nt, preferred_element_type=F32)
            if masked:
                r = lax.broadcasted_iota(jnp.int32, (tq, tq), 0)
                c = lax.broadcasted_iota(jnp.int32, (tq, tq), 1)
                s = jnp.where(c <= r, s, -jnp.inf)
            m_prev = m_ref[e]
            m_new = jnp.maximum(m_prev, jnp.max(s, axis=-1, keepdims=True))
            alpha = jnp.exp2(m_prev - m_new)
            pr = jnp.exp2(s - m_new)
            l_ref[e] = alpha * l_ref[e] + jnp.sum(pr, axis=-1, keepdims=True)
            acc_ref[e] = alpha * acc_ref[e] + jnp.dot(pr.astype(BF16), v, preferred_element_type=F32)
            m_ref[e] = m_new

    def body(kb, carry):
        block(kb, False)
        return carry

    lax.fori_loop(0, qi, body, 0)
    block(qi, True)
    lane = lax.broadcasted_iota(jnp.int32, (tq, 2 * MLA_V), 1)
    o0 = acc_ref[0] / l_ref[0]
    o1 = acc_ref[1] / l_ref[1]
    o_ref[...] = jnp.where(lane < MLA_V, o0, o1).astype(o_ref.dtype)


def _attention(q, k, v, *, batch, seq, tq=256):
    t = q.shape[0]
    nq = seq // tq
    pairs = N_HEADS // 2
    return pl.pallas_call(
        functools.partial(_attn_body, tq=tq),
        out_shape=jax.ShapeDtypeStruct((t, N_HEADS * MLA_V), BF16),
        grid=(batch, pairs, nq),
        in_specs=[
            pl.BlockSpec((tq, 2 * HEAD_PAD), lambda b, p, i: (b * nq + i, p)),
            pl.BlockSpec((seq, 2 * HEAD_PAD), lambda b, p, i: (b, p)),
            pl.BlockSpec((seq, 2 * MLA_V), lambda b, p, i: (b, p)),
        ],
        out_specs=pl.BlockSpec((tq, 2 * MLA_V), lambda b, p, i: (b * nq + i, p)),
        scratch_shapes=[pltpu.VMEM((2, tq, 1), F32), pltpu.VMEM((2, tq, 1), F32),
                        pltpu.VMEM((2, tq, 2 * MLA_V), F32)],
        compiler_params=_cparams(("arbitrary", "arbitrary", "arbitrary")),
        name="mla_attention",
    )(q, k, v)


def _split3(a):
    hi = a.astype(BF16)
    r1 = a - hi.astype(F32)
    mid = r1.astype(BF16)
    lo = (r1 - mid.astype(F32)).astype(BF16)
    return hi, mid, lo


def _split2(a):
    hi = a.astype(BF16)
    lo = (a - hi.astype(F32)).astype(BF16)
    return hi, lo


def _dot(a, b):
    return jnp.dot(a, b, preferred_element_type=F32)


def _dot_exact_rhs(a, b_bf16):
    hi, mid, lo = _split3(a)
    return _dot(hi, b_bf16) + _dot(mid, b_bf16) + _dot(lo, b_bf16)


def _dot_exact_lhs(a_bf16, b):
    hi, mid, lo = _split3(b)
    return _dot(a_bf16, hi) + _dot(a_bf16, mid) + _dot(a_bf16, lo)


def _dot_f32(a, b):
    ah, al = _split2(a)
    bh, bl = _split2(b)
    return _dot(ah, bh) + _dot(ah, bl) + _dot(al, bh)


def _gdn_body(qkv_ref, beta_ref, g_ref, z_ref, onorm_ref, o_ref, state_ref, *, tc):
    c = GDN_CHUNK
    npairs = N_HEADS // 2

    @pl.when(pl.program_id(1) == 0)
    def _():
        state_ref[...] = jnp.zeros(state_ref.shape, F32)

    ri = lax.broadcasted_iota(jnp.int32, (c, LANES), 0)
    li = lax.broadcasted_iota(jnp.int32, (c, LANES), 1)
    lj = li & (c - 1)
    tril2 = lj <= ri
    strict2 = lj < ri
    eye2 = (lj == ri).astype(F32)
    r2 = lax.broadcasted_iota(jnp.int32, (2 * c, LANES), 0)
    l2 = lax.broadcasted_iota(jnp.int32, (2 * c, LANES), 1)
    blkmask = (r2 < c) == (l2 < c)
    gsum_m = blkmask.astype(BF16)
    rr = lax.broadcasted_iota(jnp.int32, (c, c), 0)
    cc = lax.broadcasted_iota(jnp.int32, (c, c), 1)
    ltri = (cc <= rr).astype(BF16)
    first_half = li < c

    def blk(rp):
        return jnp.where(blkmask, jnp.concatenate([rp, rp], axis=0), 0.0)

    def mm(lp, rp):
        return _dot(lp.astype(BF16), blk(rp).astype(BF16))

    def mm3(lp, rp):
        return _dot_f32(lp, blk(rp))

    def gsum(y):
        hi, lo = _split2(y)
        return _dot(hi, gsum_m) + _dot(lo, gsum_m)

    nt = (((1,), (1,)), ((), ()))
    tn = (((0,), (0,)), ((), ()))
    level_masks = []
    for sh in (3, 4, 5):
        same = (ri >> (sh + 1)) == (lj >> (sh + 1))
        level_masks.append(same & (((ri >> sh) & 1) == 1) & (((lj >> sh) & 1) == 0))
    diag8 = ((ri >> 3) == (lj >> 3)) & strict2

    for ch in range(tc // c):
        rows = slice(ch * c, (ch + 1) * c)
        for p in range(npairs):
            ls = slice(p * LANES, (p + 1) * LANES)
            q = qkv_ref[rows, p * LANES:(p + 1) * LANES]
            k = qkv_ref[rows, GDN_W + p * LANES:GDN_W + (p + 1) * LANES]
            v = qkv_ref[rows, 2 * GDN_W + p * LANES:2 * GDN_W + (p + 1) * LANES]
            bt = beta_ref[rows, ls]
            gg = g_ref[rows, ls]
            qn = q * lax.rsqrt(gsum(q * q) + EPS) * (GDN_DK ** -0.5)
            kn = k * lax.rsqrt(gsum(k * k) + EPS)
            gcum = _dot_exact_lhs(ltri, gg)
            diff = _dot_exact_lhs(ltri, jnp.where(strict2, gg, 0.0))
            decay = jnp.exp(jnp.where(tril2, diff, -jnp.inf))
            egc = jnp.exp(gcum)
            kb = kn * bt
            vb = v * bt
            kblk = blk(kn).astype(BF16)
            a = lax.dot_general(kb.astype(BF16), kblk, nt, preferred_element_type=F32)
            attn = lax.dot_general(qn.astype(BF16), kblk, nt, preferred_element_type=F32) * decay
            n = jnp.where(strict2, a * decay, 0.0)
            nd = jnp.where(diag8, n, 0.0)
            n2 = mm3(nd, nd)
            n4 = mm3(n2, n2)
            tinv = mm3(eye2 - nd, eye2 + n2)
            tinv = mm3(tinv, eye2 + n4)
            for lm in level_masks:
                e = jnp.where(lm, n, 0.0)
                xk = mm3(tinv, e)
                tinv = tinv - mm3(xk, tinv)
            u = mm(tinv, vb)
            w = mm(tinv, kb * egc)
            qd = qn * egc
            glast = gcum[c - 1:c, :]
            ktail = kn * jnp.exp(glast - gcum)
            st = state_ref[p]
            vnew = u - mm(w, st)
            o = mm(qd, st) + mm(attn, vnew)
            upd = lax.dot_general(ktail.astype(BF16), vnew.astype(BF16), tn, preferred_element_type=F32)
            state_ref[p] = st * jnp.exp(glast) + jnp.where(first_half, upd[0:c, :], upd[c:2 * c, :])
            ms = gsum(o * o) * (1.0 / GDN_DV)
            zz = z_ref[rows, ls]
            y = o * lax.rsqrt(ms + EPS) * onorm_ref[...] * (zz * jax.nn.sigmoid(zz))
            o_ref[rows, ls] = y.astype(o_ref.dtype)


def _gdn(qkv, beta, g, z, onorm2, *, batch, seq, tc=256):
    t = qkv.shape[0]
    nb = seq // tc
    row = lambda b, i: (b * nb + i, 0)
    return pl.pallas_call(
        functools.partial(_gdn_body, tc=tc),
        out_shape=jax.ShapeDtypeStruct((t, GDN_W), BF16),
        grid=(batch, nb),
        in_specs=[pl.BlockSpec((tc, 3 * GDN_W), row), pl.BlockSpec((tc, GDN_W), row),
                  pl.BlockSpec((tc, GDN_W), row), pl.BlockSpec((tc, GDN_W), row),
                  _const_spec((1, LANES))],
        out_specs=pl.BlockSpec((tc, GDN_W), row),
        scratch_shapes=[pltpu.VMEM((N_HEADS // 2, GDN_DK, LANES), F32)],
        compiler_params=_cparams(("arbitrary", "arbitrary")),
        name="gated_delta_rule",
    )(qkv, beta, g, z, onorm2)


def _ab_out_body(x_ref, ya_ref, yb_ref, wa_ref, wb_ref, o_ref):
    o_ref[...] = (x_ref[...] + _dot(ya_ref[...], wa_ref[...]) + _dot(yb_ref[...], wb_ref[...]))


def _ab_out(x, ya, yb, wa, wb, *, tm=512):
    t = x.shape[0]
    row = lambda i: (i, 0)
    return pl.pallas_call(
        _ab_out_body,
        out_shape=jax.ShapeDtypeStruct((t, D_MODEL), F32),
        grid=(t // tm,),
        in_specs=[pl.BlockSpec((tm, D_MODEL), row), pl.BlockSpec((tm, ya.shape[1]), row),
                  pl.BlockSpec((tm, yb.shape[1]), row), _const_spec(wa.shape), _const_spec(wb.shape)],
        out_specs=pl.BlockSpec((tm, D_MODEL), row),
        compiler_params=_cparams(("arbitrary",)),
        name="ab_out",
    )(x, ya, yb, wa, wb)


def _sg_body(x_ref, nw_ref, win_ref, vn_ref, ws_ref, bs_ref, wout_ref, o_ref, gate_ref, *, tm):
    x = x_ref[...]
    hb = _rms(x, nw_ref[...]).astype(BF16)
    uv = _dot(hb, win_ref[...])
    uv = 0.5 * uv * (1.0 + lax.erf(uv * (2.0 ** -0.5)))
    rr = lax.broadcasted_iota(jnp.int32, (SG_CHUNK, SG_CHUNK), 0)
    cc = lax.broadcasted_iota(jnp.int32, (SG_CHUNK, SG_CHUNK), 1)
    causal = cc <= rr
    for g in range(SG_GROUPS):
        cs = slice(SG_WIDTH + g * SG_CG, SG_WIDTH + (g + 1) * SG_CG)
        vg = uv[:, cs]
        vg = vg * lax.rsqrt(jnp.mean(vg * vg, axis=-1, keepdims=True) + EPS) * vn_ref[:, g * SG_CG:(g + 1) * SG_CG]
        vg = vg.astype(BF16)
        wc = jnp.where(causal, ws_ref[g], 0.0).astype(BF16)
        bias = bs_ref[g]
        for n in range(tm // SG_CHUNK):
            rs = slice(n * SG_CHUNK, (n + 1) * SG_CHUNK)
            gate_ref[rs, g * SG_CG:(g + 1) * SG_CG] = _dot(wc, vg[rs, :]) + bias
    y = (uv[:, :SG_WIDTH] * gate_ref[...]).astype(BF16)
    o_ref[...] = x + _dot(y, wout_ref[...])


def _sg_mixer(x, nw, win, vn, ws, bs, wout, *, tm=256):
    t = x.shape[0]
    row = lambda i: (i, 0)
    consts = [nw, win, vn, ws, bs, wout]
    return pl.pallas_call(
        functools.partial(_sg_body, tm=tm),
        out_shape=jax.ShapeDtypeStruct((t, D_MODEL), F32),
        grid=(t // tm,),
        in_specs=[pl.BlockSpec((tm, D_MODEL), row)] + [_const_spec(c.shape) for c in consts],
        out_specs=pl.BlockSpec((tm, D_MODEL), row),
        scratch_shapes=[pltpu.VMEM((tm, SG_WIDTH), F32)],
        compiler_params=_cparams(("arbitrary",)),
        name="spatial_gating",
    )(x, *consts)


def _pad_heads(w, width, pad_to):
    k = w.shape[0]
    w = w.reshape(k, N_HEADS, width)
    return jnp.pad(w, ((0, 0), (0, 0), (0, pad_to - width))).reshape(k, N_HEADS * pad_to)


def _rot_partner(w_rope):
    half = MLA_ROPE // 2
    return jnp.concatenate([-w_rope[..., half:], w_rope[..., :half]], axis=-1)


def _place_rope(w_rope):
    return jnp.pad(w_rope, ((0, 0), (MLA_NOPE, LANES - MLA_QK)))


def _even_weights(l, i, norm_w, ab_w_in, mla_q_norm, mla_w_q_b, mla_kv_norm, mla_w_kv_b, mla_qk_norm_q,
                  mla_qk_norm_k, gdn_conv_w, gdn_a_log, gdn_dt_bias, gdn_out_norm, ab_w_out):
    w_in = ab_w_in[i]
    o_kr = MLA_Q_LORA + MLA_KV_LORA
    o_qkv = o_kr + MLA_ROPE
    o_z = o_qkv + 3 * GDN_W
    o_b = o_z + GDN_W
    o_a = o_b + N_HEADS
    w_kr = w_in[:, o_kr:o_qkv]
    w1 = jnp.concatenate([
        w_in[:, :o_kr],
        _place_rope(w_kr),
        _place_rope(_rot_partner(w_kr)),
        w_in[:, o_qkv:o_b],
        jnp.repeat(w_in[:, o_b:o_a], GDN_DK, axis=1),
        jnp.repeat(w_in[:, o_a:o_a + N_HEADS], GDN_DK, axis=1),
    ], axis=1).astype(BF16)
    wq = mla_w_q_b[i].reshape(MLA_Q_LORA, N_HEADS, MLA_QK)
    wq_part = jnp.concatenate([jnp.zeros_like(wq[..., :MLA_NOPE]), _rot_partner(wq[..., MLA_NOPE:])], axis=-1)
    wq2 = jnp.concatenate([
        _pad_heads(wq.reshape(MLA_Q_LORA, -1), MLA_QK, HEAD_PAD),
        _pad_heads(wq_part.reshape(MLA_Q_LORA, -1), MLA_QK, HEAD_PAD),
    ], axis=1).astype(BF16)
    wkv = mla_w_kv_b[i].reshape(MLA_KV_LORA, N_HEADS, MLA_NOPE + MLA_V)
    wkv2 = jnp.concatenate([
        _pad_heads(wkv[..., :MLA_NOPE].reshape(MLA_KV_LORA, -1), MLA_NOPE, HEAD_PAD),
        wkv[..., MLA_NOPE:].reshape(MLA_KV_LORA, -1),
    ], axis=1).astype(BF16)

    def norm_pair(nvec):
        full = jnp.pad(nvec, (0, LANES - MLA_QK)).reshape(1, LANES)
        rope = nvec[MLA_NOPE:]
        half = MLA_ROPE // 2
        part = jnp.concatenate([rope[half:], rope[:half]])
        part = jnp.pad(part, (MLA_NOPE, LANES - MLA_QK)).reshape(1, LANES)
        return full, part

    nq, nqp = norm_pair(mla_qk_norm_q[i])
    nk, nkp = norm_pair(mla_qk_norm_k[i])
    w_out = ab_w_out[i].astype(BF16)
    return dict(
        nw=norm_w[l, 1].reshape(1, D_MODEL), w1=w1, q_norm=mla_q_norm[i].reshape(1, -1), wq=wq2,
        kv_norm=mla_kv_norm[i].reshape(1, -1), wkv=wkv2, nq=nq, nqp=nqp, nk=nk, nkp=nkp,
        conv_w=gdn_conv_w[i], a_log=jnp.repeat(gdn_a_log[i], GDN_DK).reshape(1, GDN_W),
        dt_bias=jnp.repeat(gdn_dt_bias[i], GDN_DK).reshape(1, GDN_W),
        onorm2=jnp.tile(gdn_out_norm[i], 2).reshape(1, LANES),
        w_out_a=w_out[:N_HEADS * MLA_V], w_out_b=w_out[N_HEADS * MLA_V:],
    )


def kernel(x, positions, norm_w, ffn_w_gate, ffn_w_up, ffn_w_down, ab_w_in, mla_q_norm, mla_w_q_b, mla_kv_norm, mla_w_kv_b, mla_qk_norm_q, mla_qk_norm_k, gdn_conv_w, gdn_a_log, gdn_dt_bias, gdn_out_norm, ab_w_out, sg_w_in, sg_v_norm, sg_w_s, sg_b_s, sg_w_out):
    batch, seq, _ = x.shape
    depth = norm_w.shape[0]
    t = batch * seq
    xf = x.reshape(t, D_MODEL)
    wg = ffn_w_gate.astype(BF16)
    wu = ffn_w_up.astype(BF16)
    wd = ffn_w_down.astype(BF16)
    cos_full, sin_full = _rope_tables(positions)
    for l in range(depth):
        i = l // 2
        xf = _ffn(xf, norm_w[l, 0].reshape(1, D_MODEL), wg[l, 0], wu[l, 0], wd[l, 0])
        if l % 2 == 0:
            wts = _even_weights(l, i, norm_w, ab_w_in, mla_q_norm, mla_w_q_b, mla_kv_norm, mla_w_kv_b,
                                mla_qk_norm_q, mla_qk_norm_k, gdn_conv_w, gdn_a_log, gdn_dt_bias,
                                gdn_out_norm, ab_w_out)
            q, k, v, qkv, beta, g, z = _ab_in(xf, cos_full, sin_full, wts, seq=seq)
            y_mla = _attention(q, k, v, batch=batch, seq=seq)
            y_gdn = _gdn(qkv, beta, g, z, wts["onorm2"], batch=batch, seq=seq)
            xf = _ab_out(xf, y_mla, y_gdn, wts["w_out_a"], wts["w_out_b"])
        else:
            bs = jnp.broadcast_to(sg_b_s[i][:, :, None], (SG_GROUPS, SG_CHUNK, SG_CG))
            xf = _sg_mixer(xf, norm_w[l, 1].reshape(1, D_MODEL), sg_w_in[i].astype(BF16),
                           sg_v_norm[i].reshape(1, SG_WIDTH), sg_w_s[i], bs, sg_w_out[i].astype(BF16))
        xf = _ffn(xf, norm_w[l, 2].reshape(1, D_MODEL), wg[l, 1], wu[l, 1], wd[l, 1])
    return xf.reshape(batch, seq, D_MODEL)
```

```python
import functools

import numpy as np
import jax
import jax.numpy as jnp
from jax import lax
from jax.experimental import pallas as pl
from jax.experimental.pallas import tpu as pltpu

F32 = jnp.float32
BF16 = jnp.bfloat16

D_MODEL = 1024
D_FF = 2816
EPS = 1e-6
N_HEADS = 8
MLA_Q_LORA = 256
MLA_KV_LORA = 128
MLA_NOPE = 64
MLA_ROPE = 32
MLA_V = 64
MLA_QK = MLA_NOPE + MLA_ROPE
ROPE_BASE = 10000.0
GDN_DK = 64
GDN_DV = 64
GDN_CONV = 4
GDN_CHUNK = 64
GDN_W = N_HEADS * GDN_DK
SG_GROUPS = 8
SG_WIDTH = 2048
SG_CHUNK = 128
SG_CG = SG_WIDTH // SG_GROUPS

LANES = 128
VMEM_LIMIT_BYTES = 56 * 1024 * 1024
HEAD_PAD = LANES
_C_Q0 = 0
_C_KV0 = _C_Q0 + MLA_Q_LORA
_C_KR0 = _C_KV0 + MLA_KV_LORA
_C_KRP0 = _C_KR0 + LANES
_C_QKV0 = _C_KRP0 + LANES
_C_Z0 = _C_QKV0 + 3 * GDN_W
_C_B0 = _C_Z0 + GDN_W
_C_A0 = _C_B0 + GDN_W
IN_AB_PAD = _C_A0 + GDN_W

_Q_FOLD = (MLA_QK ** -0.5) * float(np.log2(np.e))


def _cparams(sem):
    return pltpu.CompilerParams(dimension_semantics=sem, vmem_limit_bytes=VMEM_LIMIT_BYTES)


def _rms(x, w):
    return x * lax.rsqrt(jnp.mean(x * x, axis=-1, keepdims=True) + EPS) * w


def _const_spec(shape):
    nd = len(shape)
    return pl.BlockSpec(shape, lambda *_: (0,) * nd, pipeline_mode=pl.Buffered(1))


def _ffn_body(x_ref, nw_ref, wg_ref, wu_ref, wd_ref, o_ref, *, fc):
    x = x_ref[...]
    h = _rms(x, nw_ref[...]).astype(BF16)
    acc = None
    for c in range(D_FF // fc):
        sl = slice(c * fc, (c + 1) * fc)
        g = jnp.dot(h, wg_ref[:, sl], preferred_element_type=F32)
        u = jnp.dot(h, wu_ref[:, sl], preferred_element_type=F32)
        a = (g * jax.nn.sigmoid(g) * u).astype(BF16)
        d = jnp.dot(a, wd_ref[sl, :], preferred_element_type=F32)
        acc = d if acc is None else acc + d
    o_ref[...] = x + 0.5 * acc


def _ffn(x, nw, wg, wu, wd, *, tm=512, fc=1408):
    t = x.shape[0]
    return pl.pallas_call(
        functools.partial(_ffn_body, fc=fc),
        out_shape=jax.ShapeDtypeStruct((t, D_MODEL), F32),
        grid=(t // tm,),
        in_specs=[
            pl.BlockSpec((tm, D_MODEL), lambda i: (i, 0)),
            _const_spec((1, D_MODEL)),
            _const_spec((D_MODEL, D_FF)),
            _const_spec((D_MODEL, D_FF)),
            _const_spec((D_FF, D_MODEL)),
        ],
        out_specs=pl.BlockSpec((tm, D_MODEL), lambda i: (i, 0)),
        compiler_params=_cparams(("arbitrary",)),
        name="ffn",
    )(x, nw, wg, wu, wd)


def _rope_body(pos_ref, invf_ref, cos_ref, sin_ref):
    ang = pos_ref[...] * invf_ref[...]
    cos_ref[...] = jnp.cos(ang)
    sin_ref[...] = jnp.sin(ang)


def _rope_tables(positions):
    b, s = positions.shape
    t = b * s
    half = MLA_ROPE // 2
    inv_freq = ROPE_BASE ** (-jnp.arange(0, MLA_ROPE, 2, dtype=F32) / MLA_ROPE)
    per_row = LANES // half
    pos = jnp.repeat(positions.reshape(t).astype(F32), half).reshape(t // per_row, LANES)
    invf = jnp.tile(inv_freq, per_row).reshape(1, LANES)
    rows = t // per_row
    tr = min(rows, 512)
    cos, sin = pl.pallas_call(
        _rope_body,
        out_shape=(jax.ShapeDtypeStruct((rows, LANES), F32),) * 2,
        grid=(rows // tr,),
        in_specs=[pl.BlockSpec((tr, LANES), lambda i: (i, 0)), _const_spec((1, LANES))],
        out_specs=(pl.BlockSpec((tr, LANES), lambda i: (i, 0)),) * 2,
        compiler_params=_cparams(("arbitrary",)),
        name="rope_tables",
    )(pos, invf)
    cos = cos.reshape(t, half)
    sin = sin.reshape(t, half)
    pad = LANES - MLA_QK
    cos_full = jnp.concatenate([jnp.ones((t, MLA_NOPE), F32), cos, cos, jnp.ones((t, pad), F32)], axis=-1)
    sin_full = jnp.concatenate([jnp.zeros((t, MLA_NOPE), F32), sin, sin, jnp.zeros((t, pad), F32)], axis=-1)
    return cos_full, sin_full


def _softplus(x):
    return jnp.maximum(x, 0.0) + jnp.log1p(jnp.exp(-jnp.abs(x)))


def _ab_in_body(x_ref, cos_ref, sin_ref, nw_ref, w1_ref, qn_ref, wq_ref, kvn_ref, wkv_ref, wvt_ref,
                nq_ref, nqp_ref, nk_ref, nkp_ref, convw_ref, alog_ref, dtb_ref,
                q_out, k_out, vt_out, qkv_out, beta_out, g_out, z_out,
                carry_ref, *, tm, tiles_per_seq):
    i = pl.program_id(0)
    x = x_ref[...]
    hb = _rms(x, nw_ref[...]).astype(BF16)
    p = jnp.dot(hb, w1_ref[...], preferred_element_type=F32)

    c_q = p[:, _C_Q0:_C_KV0]
    c_kv = p[:, _C_KV0:_C_KR0]
    kr = p[:, _C_KR0:_C_KRP0]
    krp = p[:, _C_KRP0:_C_QKV0]
    cqn = _rms(c_q, qn_ref[...]).astype(BF16)
    qq = jnp.dot(cqn, wq_ref[...], preferred_element_type=F32)
    ckvn = _rms(c_kv, kvn_ref[...]).astype(BF16)
    kk = jnp.dot(ckvn, wkv_ref[...], preferred_element_type=F32)
    vt = lax.dot_general(wvt_ref[...], ckvn, (((1,), (1,)), ((), ())), preferred_element_type=F32)
    vt_out[0] = vt.astype(BF16)
    cos = cos_ref[...]
    sin = sin_ref[...]
    hw = N_HEADS * HEAD_PAD
    for h in range(N_HEADS):
        sl = slice(h * HEAD_PAD, (h + 1) * HEAD_PAD)
        qh = qq[:, sl]
        qph = qq[:, hw + h * HEAD_PAD: hw + (h + 1) * HEAD_PAD]
        rq = lax.rsqrt(jnp.sum(qh * qh, axis=-1, keepdims=True) * (1.0 / MLA_QK) + EPS)
        qr = (qh * nq_ref[...] * cos + qph * nqp_ref[...] * sin) * (rq * _Q_FOLD)
        q_out[:, sl] = qr.astype(BF16)
        kh = kk[:, sl] + kr
        rk = lax.rsqrt(jnp.sum(kh * kh, axis=-1, keepdims=True) * (1.0 / MLA_QK) + EPS)
        kro = (kh * nk_ref[...] * cos + krp * nkp_ref[...] * sin) * rk
        k_out[:, sl] = kro.astype(BF16)

    qkv_pre = p[:, _C_QKV0:_C_Z0]

    @pl.when(i % tiles_per_seq == 0)
    def _():
        carry_ref[0:8, :] = jnp.zeros((8, 3 * GDN_W), F32)

    carry_ref[8:8 + tm, :] = qkv_pre
    conv = qkv_pre * convw_ref[GDN_CONV - 1:GDN_CONV, :]
    for j in range(GDN_CONV - 1):
        off = 8 - (GDN_CONV - 1) + j
        conv = conv + carry_ref[off:off + tm, :] * convw_ref[j:j + 1, :]
    carry_ref[0:8, :] = qkv_pre[tm - 8:tm, :]
    qkv_out[...] = conv * jax.nn.sigmoid(conv)
    beta_out[...] = jax.nn.sigmoid(p[:, _C_B0:_C_A0])
    g_out[...] = -jnp.exp(alog_ref[...]) * _softplus(p[:, _C_A0:IN_AB_PAD] + dtb_ref[...])
    z_out[...] = p[:, _C_Z0:_C_B0]


def _ab_in(x, cos_full, sin_full, wts, *, seq, tm=256):
    t = x.shape[0]
    tiles_per_seq = seq // tm
    row = lambda i: (i, 0)
    hw = N_HEADS * HEAD_PAD
    outs = (
        jax.ShapeDtypeStruct((t, hw), BF16),
        jax.ShapeDtypeStruct((t, hw), BF16),
        jax.ShapeDtypeStruct((t // tm, N_HEADS * MLA_V, tm), BF16),
        jax.ShapeDtypeStruct((t, 3 * GDN_W), F32),
        jax.ShapeDtypeStruct((t, GDN_W), F32),
        jax.ShapeDtypeStruct((t, GDN_W), F32),
        jax.ShapeDtypeStruct((t, GDN_W), F32),
    )
    consts = [wts["nw"], wts["w1"], wts["q_norm"], wts["wq"], wts["kv_norm"], wts["wkv"], wts["wvt"],
              wts["nq"], wts["nqp"], wts["nk"], wts["nkp"], wts["conv_w"], wts["a_log"], wts["dt_bias"]]
    out_specs = tuple(
        pl.BlockSpec((1,) + o.shape[1:], lambda i: (i, 0, 0)) if len(o.shape) == 3
        else pl.BlockSpec((tm, o.shape[1]), row) for o in outs)
    return pl.pallas_call(
        functools.partial(_ab_in_body, tm=tm, tiles_per_seq=tiles_per_seq),
        out_shape=outs,
        grid=(t // tm,),
        in_specs=[pl.BlockSpec((tm, D_MODEL), row), pl.BlockSpec((tm, LANES), row),
                  pl.BlockSpec((tm, LANES), row)] + [_const_spec(c.shape) for c in consts],
        out_specs=out_specs,
        scratch_shapes=[pltpu.VMEM((tm + 8, 3 * GDN_W), F32)],
        compiler_params=_cparams(("arbitrary",)),
        name="ab_in",
    )(x, cos_full, sin_full, *consts)


def _attn_body(q_ref, k_ref, vt_ref, o_ref, m_ref, l_ref, acc_ref, *, tq, vtile, nh):
    qi = pl.program_id(2)
    nt = (((1,), (1,)), ((), ()))
    sub = tq // vtile
    pair_rows = 2 * MLA_V
    m_ref[...] = jnp.full(m_ref.shape, -jnp.inf, F32)
    l_ref[...] = jnp.zeros(l_ref.shape, F32)
    acc_ref[...] = jnp.zeros(acc_ref.shape, F32)

    def block(kb, masked):
        row0 = pl.multiple_of(kb * tq, tq)
        st, pt, alpha = [], [], []
        for e in range(nh):
            q = q_ref[:, e * HEAD_PAD:(e + 1) * HEAD_PAD]
            k = k_ref[pl.ds(row0, tq), e * HEAD_PAD:(e + 1) * HEAD_PAD]
            s = lax.dot_general(k, q, nt, preferred_element_type=F32)
            if masked:
                kr = lax.broadcasted_iota(jnp.int32, (tq, tq), 0)
                qc = lax.broadcasted_iota(jnp.int32, (tq, tq), 1)
                s = jnp.where(kr <= qc, s, -jnp.inf)
            st.append(s)
        for e in range(nh):
            m_prev = m_ref[e]
            m_new = jnp.maximum(m_prev, jnp.max(st[e], axis=0, keepdims=True))
            a = jnp.exp2(m_prev - m_new)
            p = jnp.exp2(st[e] - m_new)
            l_ref[e] = a * l_ref[e] + jnp.sum(p, axis=0, keepdims=True)
            m_ref[e] = m_new
            alpha.append(a)
            pt.append(p.astype(BF16))
        for e in range(nh):
            pr = e // 2
            pv = None
            for j in range(sub):
                vt = vt_ref[kb * sub + j, pr * pair_rows:(pr + 1) * pair_rows, :]
                d = _dot(vt, pt[e][j * vtile:(j + 1) * vtile, :])
                pv = d if pv is None else pv + d
            acc_ref[e] = alpha[e] * acc_ref[e] + pv

    def body(kb, carry):
        block(kb, False)
        return carry

    lax.fori_loop(0, qi, body, 0)
    block(qi, True)
    feat = lax.broadcasted_iota(jnp.int32, (pair_rows, tq), 0)
    for pr in range(nh // 2):
        o0 = acc_ref[2 * pr] * (1.0 / l_ref[2 * pr])
        o1 = acc_ref[2 * pr + 1] * (1.0 / l_ref[2 * pr + 1])
        o_ref[:, pr * pair_rows:(pr + 1) * pair_rows] = jnp.where(feat < MLA_V, o0, o1).T.astype(o_ref.dtype)


def _attention(q, k, vt, *, batch, seq, tq=512, nh=4):
    t = q.shape[0]
    vtile = vt.shape[2]
    nq = seq // tq
    nv = seq // vtile
    return pl.pallas_call(
        functools.partial(_attn_body, tq=tq, vtile=vtile, nh=nh),
        out_shape=jax.ShapeDtypeStruct((t, N_HEADS * MLA_V), BF16),
        grid=(batch, N_HEADS // nh, nq),
        in_specs=[
            pl.BlockSpec((tq, nh * HEAD_PAD), lambda b, p, i: (b * nq + i, p)),
            pl.BlockSpec((seq, nh * HEAD_PAD), lambda b, p, i: (b, p)),
            pl.BlockSpec((nv, nh * MLA_V, vtile), lambda b, p, i: (b, p, 0)),
        ],
        out_specs=pl.BlockSpec((tq, nh * MLA_V), lambda b, p, i: (b * nq + i, p)),
        scratch_shapes=[pltpu.VMEM((nh, 1, tq), F32), pltpu.VMEM((nh, 1, tq), F32),
                        pltpu.VMEM((nh, 2 * MLA_V, tq), F32)],
        compiler_params=_cparams(("arbitrary", "arbitrary", "arbitrary")),
        name="mla_attention",
    )(q, k, vt)


def _split3(a):
    hi = a.astype(BF16)
    r1 = a - hi.astype(F32)
    mid = r1.astype(BF16)
    lo = (r1 - mid.astype(F32)).astype(BF16)
    return hi, mid, lo


def _split2(a):
    hi = a.astype(BF16)
    lo = (a - hi.astype(F32)).astype(BF16)
    return hi, lo


def _dot(a, b):
    return jnp.dot(a, b, preferred_element_type=F32)


def _dot_exact_lhs(a_bf16, b):
    hi, mid, lo = _split3(b)
    return _dot(a_bf16, hi) + _dot(a_bf16, mid) + _dot(a_bf16, lo)


def _gdn_body(qkv_ref, beta_ref, g_ref, z_ref, onorm_ref, o_ref, state_ref, *, tc):
    c = GDN_CHUNK
    npairs = N_HEADS // 2

    @pl.when(pl.program_id(1) == 0)
    def _():
        state_ref[...] = jnp.zeros(state_ref.shape, F32)

    ri = lax.broadcasted_iota(jnp.int32, (c, LANES), 0)
    li = lax.broadcasted_iota(jnp.int32, (c, LANES), 1)
    lj = li & (c - 1)
    tril2 = lj <= ri
    strict2 = lj < ri
    eye2 = (lj == ri).astype(F32)
    r2 = lax.broadcasted_iota(jnp.int32, (2 * c, LANES), 0)
    l2 = lax.broadcasted_iota(jnp.int32, (2 * c, LANES), 1)
    blkmask = (r2 < c) == (l2 < c)
    gsum_m = blkmask.astype(BF16)
    rr = lax.broadcasted_iota(jnp.int32, (c, c), 0)
    cc = lax.broadcasted_iota(jnp.int32, (c, c), 1)
    ltri = (cc <= rr).astype(BF16)
    first_half = li < c

    def blk(rp):
        return jnp.where(blkmask, jnp.concatenate([rp, rp], axis=0), 0.0)

    def mm(lp, rp):
        return _dot(lp.astype(BF16), blk(rp).astype(BF16))

    mmi = mm

    def gsum(y):
        hi, lo = _split2(y)
        return _dot(hi, gsum_m) + _dot(lo, gsum_m)

    nt = (((1,), (1,)), ((), ()))
    tn = (((0,), (0,)), ((), ()))
    level_masks = []
    for sh in (3, 4, 5):
        same = (ri >> (sh + 1)) == (lj >> (sh + 1))
        level_masks.append(same & (((ri >> sh) & 1) == 1) & (((lj >> sh) & 1) == 0))
    diag8 = ((ri >> 3) == (lj >> 3)) & strict2

    nch = tc // c
    probs = [(ch, p) for ch in range(nch) for p in range(npairs)]
    rows = lambda ch: slice(ch * c, (ch + 1) * c)
    ls = lambda p: slice(p * LANES, (p + 1) * LANES)
    qn, kn, kb, vb, gcum, decay, egc = {}, {}, {}, {}, {}, {}, {}
    for pr in probs:
        ch, p = pr
        q = qkv_ref[rows(ch), p * LANES:(p + 1) * LANES]
        k = qkv_ref[rows(ch), GDN_W + p * LANES:GDN_W + (p + 1) * LANES]
        v = qkv_ref[rows(ch), 2 * GDN_W + p * LANES:2 * GDN_W + (p + 1) * LANES]
        bt = beta_ref[rows(ch), ls(p)]
        gg = g_ref[rows(ch), ls(p)]
        qn[pr] = q * lax.rsqrt(gsum(q * q) + EPS) * (GDN_DK ** -0.5)
        kn[pr] = k * lax.rsqrt(gsum(k * k) + EPS)
        cs = _dot_exact_lhs(ltri, jnp.concatenate([gg, jnp.where(strict2, gg, 0.0)], axis=1))
        gcum[pr] = cs[:, :LANES]
        decay[pr] = jnp.exp(jnp.where(tril2, cs[:, LANES:], -jnp.inf))
        egc[pr] = jnp.exp(gcum[pr])
        kb[pr] = kn[pr] * bt
        vb[pr] = v * bt
    n, attn = {}, {}
    for pr in probs:
        lhs = jnp.concatenate([kb[pr], qn[pr]], axis=0).astype(BF16)
        res = lax.dot_general(lhs, blk(kn[pr]).astype(BF16), nt, preferred_element_type=F32)
        n[pr] = jnp.where(strict2, res[:c] * decay[pr], 0.0)
        attn[pr] = res[c:] * decay[pr]
    nd = {pr: jnp.where(diag8, n[pr], 0.0) for pr in probs}
    n2 = {pr: mmi(nd[pr], nd[pr]) for pr in probs}
    n4 = {pr: mmi(n2[pr], n2[pr]) for pr in probs}
    tinv = {pr: mmi(eye2 - nd[pr], eye2 + n2[pr]) for pr in probs}
    tinv = {pr: mmi(tinv[pr], eye2 + n4[pr]) for pr in probs}
    for lm in level_masks:
        xk = {pr: mmi(tinv[pr], jnp.where(lm, n[pr], 0.0)) for pr in probs}
        tinv = {pr: tinv[pr] - mmi(xk[pr], tinv[pr]) for pr in probs}
    u, w = {}, {}
    for pr in probs:
        rhs = jnp.concatenate([blk(vb[pr]), blk(kb[pr] * egc[pr])], axis=1).astype(BF16)
        uw = _dot(tinv[pr].astype(BF16), rhs)
        u[pr] = uw[:, :LANES]
        w[pr] = uw[:, LANES:]
    st = [state_ref[p] for p in range(npairs)]
    for ch in range(nch):
        for p in range(npairs):
            pr = (ch, p)
            glast = gcum[pr][c - 1:c, :]
            ktail = kn[pr] * jnp.exp(glast - gcum[pr])
            wq = jnp.concatenate([w[pr], qn[pr] * egc[pr]], axis=0).astype(BF16)
            ws = _dot(wq, blk(st[p]).astype(BF16))
            vnew = u[pr] - ws[:c]
            o = ws[c:] + mm(attn[pr], vnew)
            upd = lax.dot_general(ktail.astype(BF16), vnew.astype(BF16), tn, preferred_element_type=F32)
            st[p] = st[p] * jnp.exp(glast) + jnp.where(first_half, upd[0:c, :], upd[c:2 * c, :])
            ms = gsum(o * o) * (1.0 / GDN_DV)
            zz = z_ref[rows(ch), ls(p)]
            y = o * lax.rsqrt(ms + EPS) * onorm_ref[...] * (zz * jax.nn.sigmoid(zz))
            o_ref[rows(ch), ls(p)] = y.astype(o_ref.dtype)
    for p in range(npairs):
        state_ref[p] = st[p]


def _gdn(qkv, beta, g, z, onorm2, *, batch, seq, tc=256):
    t = qkv.shape[0]
    nb = seq // tc
    row = lambda b, i: (b * nb + i, 0)
    return pl.pallas_call(
        functools.partial(_gdn_body, tc=tc),
        out_shape=jax.ShapeDtypeStruct((t, GDN_W), BF16),
        grid=(batch, nb),
        in_specs=[pl.BlockSpec((tc, 3 * GDN_W), row), pl.BlockSpec((tc, GDN_W), row),
                  pl.BlockSpec((tc, GDN_W), row), pl.BlockSpec((tc, GDN_W), row),
                  _const_spec((1, LANES))],
        out_specs=pl.BlockSpec((tc, GDN_W), row),
        scratch_shapes=[pltpu.VMEM((N_HEADS // 2, GDN_DK, LANES), F32)],
        compiler_params=_cparams(("arbitrary", "arbitrary")),
        name="gated_delta_rule",
    )(qkv, beta, g, z, onorm2)


def _ab_out_body(x_ref, ya_ref, yb_ref, wa_ref, wb_ref, o_ref):
    o_ref[...] = (x_ref[...] + _dot(ya_ref[...], wa_ref[...]) + _dot(yb_ref[...], wb_ref[...]))


def _ab_out(x, ya, yb, wa, wb, *, tm=512):
    t = x.shape[0]
    row = lambda i: (i, 0)
    return pl.pallas_call(
        _ab_out_body,
        out_shape=jax.ShapeDtypeStruct((t, D_MODEL), F32),
        grid=(t // tm,),
        in_specs=[pl.BlockSpec((tm, D_MODEL), row), pl.BlockSpec((tm, ya.shape[1]), row),
                  pl.BlockSpec((tm, yb.shape[1]), row), _const_spec(wa.shape), _const_spec(wb.shape)],
        out_specs=pl.BlockSpec((tm, D_MODEL), row),
        compiler_params=_cparams(("arbitrary",)),
        name="ab_out",
    )(x, ya, yb, wa, wb)


def _sg_body(x_ref, nw_ref, win_ref, vn_ref, ws_ref, bs_ref, wout_ref, o_ref, gate_ref, *, tm):
    x = x_ref[...]
    hb = _rms(x, nw_ref[...]).astype(BF16)
    uv = _dot(hb, win_ref[...])
    uv = 0.5 * uv * (1.0 + lax.erf(uv * (2.0 ** -0.5)))
    rr = lax.broadcasted_iota(jnp.int32, (SG_CHUNK, SG_CHUNK), 0)
    cc = lax.broadcasted_iota(jnp.int32, (SG_CHUNK, SG_CHUNK), 1)
    causal = cc <= rr
    for g in range(SG_GROUPS):
        cs = slice(SG_WIDTH + g * SG_CG, SG_WIDTH + (g + 1) * SG_CG)
        vg = uv[:, cs]
        vg = vg * lax.rsqrt(jnp.mean(vg * vg, axis=-1, keepdims=True) + EPS) * vn_ref[:, g * SG_CG:(g + 1) * SG_CG]
        vg = vg.astype(BF16)
        wc = jnp.where(causal, ws_ref[g], 0.0).astype(BF16)
        bias = bs_ref[g]
        for n in range(tm // SG_CHUNK):
            rs = slice(n * SG_CHUNK, (n + 1) * SG_CHUNK)
            gate_ref[rs, g * SG_CG:(g + 1) * SG_CG] = _dot(wc, vg[rs, :]) + bias
    y = (uv[:, :SG_WIDTH] * gate_ref[...]).astype(BF16)
    o_ref[...] = x + _dot(y, wout_ref[...])


def _sg_mixer(x, nw, win, vn, ws, bs, wout, *, tm=256):
    t = x.shape[0]
    row = lambda i: (i, 0)
    consts = [nw, win, vn, ws, bs, wout]
    return pl.pallas_call(
        functools.partial(_sg_body, tm=tm),
        out_shape=jax.ShapeDtypeStruct((t, D_MODEL), F32),
        grid=(t // tm,),
        in_specs=[pl.BlockSpec((tm, D_MODEL), row)] + [_const_spec(c.shape) for c in consts],
        out_specs=pl.BlockSpec((tm, D_MODEL), row),
        scratch_shapes=[pltpu.VMEM((tm, SG_WIDTH), F32)],
        compiler_params=_cparams(("arbitrary",)),
        name="spatial_gating",
    )(x, *consts)


def _pad_heads(w, width, pad_to):
    k = w.shape[0]
    w = w.reshape(k, N_HEADS, width)
    return jnp.pad(w, ((0, 0), (0, 0), (0, pad_to - width))).reshape(k, N_HEADS * pad_to)


def _rot_partner(w_rope):
    half = MLA_ROPE // 2
    return jnp.concatenate([-w_rope[..., half:], w_rope[..., :half]], axis=-1)


def _place_rope(w_rope):
    return jnp.pad(w_rope, ((0, 0), (MLA_NOPE, LANES - MLA_QK)))


def _even_weights(l, i, norm_w, ab_w_in, mla_q_norm, mla_w_q_b, mla_kv_norm, mla_w_kv_b, mla_qk_norm_q,
                  mla_qk_norm_k, gdn_conv_w, gdn_a_log, gdn_dt_bias, gdn_out_norm, ab_w_out):
    w_in = ab_w_in[i]
    o_kr = MLA_Q_LORA + MLA_KV_LORA
    o_qkv = o_kr + MLA_ROPE
    o_z = o_qkv + 3 * GDN_W
    o_b = o_z + GDN_W
    o_a = o_b + N_HEADS
    w_kr = w_in[:, o_kr:o_qkv]
    w1 = jnp.concatenate([
        w_in[:, :o_kr],
        _place_rope(w_kr),
        _place_rope(_rot_partner(w_kr)),
        w_in[:, o_qkv:o_b],
        jnp.repeat(w_in[:, o_b:o_a], GDN_DK, axis=1),
        jnp.repeat(w_in[:, o_a:o_a + N_HEADS], GDN_DK, axis=1),
    ], axis=1).astype(BF16)
    wq = mla_w_q_b[i].reshape(MLA_Q_LORA, N_HEADS, MLA_QK)
    wq_part = jnp.concatenate([jnp.zeros_like(wq[..., :MLA_NOPE]), _rot_partner(wq[..., MLA_NOPE:])], axis=-1)
    wq2 = jnp.concatenate([
        _pad_heads(wq.reshape(MLA_Q_LORA, -1), MLA_QK, HEAD_PAD),
        _pad_heads(wq_part.reshape(MLA_Q_LORA, -1), MLA_QK, HEAD_PAD),
    ], axis=1).astype(BF16)
    wkv = mla_w_kv_b[i].reshape(MLA_KV_LORA, N_HEADS, MLA_NOPE + MLA_V)
    wkv2 = _pad_heads(wkv[..., :MLA_NOPE].reshape(MLA_KV_LORA, -1), MLA_NOPE, HEAD_PAD).astype(BF16)
    wvt = wkv[..., MLA_NOPE:].reshape(MLA_KV_LORA, -1).T.astype(BF16)

    def norm_pair(nvec):
        full = jnp.pad(nvec, (0, LANES - MLA_QK)).reshape(1, LANES)
        rope = nvec[MLA_NOPE:]
        half = MLA_ROPE // 2
        part = jnp.concatenate([rope[half:], rope[:half]])
        part = jnp.pad(part, (MLA_NOPE, LANES - MLA_QK)).reshape(1, LANES)
        return full, part

    nq, nqp = norm_pair(mla_qk_norm_q[i])
    nk, nkp = norm_pair(mla_qk_norm_k[i])
    w_out = ab_w_out[i].astype(BF16)
    return dict(
        nw=norm_w[l, 1].reshape(1, D_MODEL), w1=w1, q_norm=mla_q_norm[i].reshape(1, -1), wq=wq2,
        kv_norm=mla_kv_norm[i].reshape(1, -1), wkv=wkv2, wvt=wvt, nq=nq, nqp=nqp, nk=nk, nkp=nkp,
        conv_w=gdn_conv_w[i], a_log=jnp.repeat(gdn_a_log[i], GDN_DK).reshape(1, GDN_W),
        dt_bias=jnp.repeat(gdn_dt_bias[i], GDN_DK).reshape(1, GDN_W),
        onorm2=jnp.tile(gdn_out_norm[i], 2).reshape(1, LANES),
        w_out_a=w_out[:N_HEADS * MLA_V], w_out_b=w_out[N_HEADS * MLA_V:],
    )


def kernel(x, positions, norm_w, ffn_w_gate, ffn_w_up, ffn_w_down, ab_w_in, mla_q_norm, mla_w_q_b, mla_kv_norm, mla_w_kv_b, mla_qk_norm_q, mla_qk_norm_k, gdn_conv_w, gdn_a_log, gdn_dt_bias, gdn_out_norm, ab_w_out, sg_w_in, sg_v_norm, sg_w_s, sg_b_s, sg_w_out):
    batch, seq, _ = x.shape
    depth = norm_w.shape[0]
    t = batch * seq
    xf = x.reshape(t, D_MODEL)
    wg = ffn_w_gate.astype(BF16)
    wu = ffn_w_up.astype(BF16)
    wd = ffn_w_down.astype(BF16)
    cos_full, sin_full = _rope_tables(positions)
    for l in range(depth):
        i = l // 2
        xf = _ffn(xf, norm_w[l, 0].reshape(1, D_MODEL), wg[l, 0], wu[l, 0], wd[l, 0])
        if l % 2 == 0:
            wts = _even_weights(l, i, norm_w, ab_w_in, mla_q_norm, mla_w_q_b, mla_kv_norm, mla_w_kv_b,
                                mla_qk_norm_q, mla_qk_norm_k, gdn_conv_w, gdn_a_log, gdn_dt_bias,
                                gdn_out_norm, ab_w_out)
            q, k, vt, qkv, beta, g, z = _ab_in(xf, cos_full, sin_full, wts, seq=seq)
            y_mla = _attention(q, k, vt, batch=batch, seq=seq)
            y_gdn = _gdn(qkv, beta, g, z, wts["onorm2"], batch=batch, seq=seq)
            xf = _ab_out(xf, y_mla, y_gdn, wts["w_out_a"], wts["w_out_b"])
        else:
            bs = jnp.broadcast_to(sg_b_s[i][:, :, None], (SG_GROUPS, SG_CHUNK, SG_CG))
            xf = _sg_mixer(xf, norm_w[l, 1].reshape(1, D_MODEL), sg_w_in[i].astype(BF16),
                           sg_v_norm[i].reshape(1, SG_WIDTH), sg_w_s[i], bs, sg_w_out[i].astype(BF16))
        xf = _ffn(xf, norm_w[l, 2].reshape(1, D_MODEL), wg[l, 1], wu[l, 1], wd[l, 1])
    return xf.reshape(batch, seq, D_MODEL)
```

```python
import functools

import numpy as np
import jax
import jax.numpy as jnp
from jax import lax
from jax.experimental import pallas as pl
from jax.experimental.pallas import tpu as pltpu

F32 = jnp.float32
BF16 = jnp.bfloat16

D_MODEL = 1024
D_FF = 2816
EPS = 1e-6
N_HEADS = 8
MLA_Q_LORA = 256
MLA_KV_LORA = 128
MLA_NOPE = 64
MLA_ROPE = 32
MLA_V = 64
MLA_QK = MLA_NOPE + MLA_ROPE
ROPE_BASE = 10000.0
GDN_DK = 64
GDN_DV = 64
GDN_CONV = 4
GDN_CHUNK = 64
GDN_W = N_HEADS * GDN_DK
SG_GROUPS = 8
SG_WIDTH = 2048
SG_CHUNK = 128
SG_CG = SG_WIDTH // SG_GROUPS

LANES = 128
MXU_TILE = 256
VMEM_LIMIT_BYTES = 56 * 1024 * 1024
HEAD_PAD = LANES
VT_ROWS = 128
AB_SUB = 256
_C_Q0 = 0
_C_KV0 = _C_Q0 + MLA_Q_LORA
_C_KR0 = _C_KV0 + MLA_KV_LORA
_C_KRP0 = _C_KR0 + LANES
_C_QKV0 = _C_KRP0 + LANES
_C_Z0 = _C_QKV0 + 3 * GDN_W
_C_B0 = _C_Z0 + GDN_W
_C_A0 = _C_B0 + GDN_W
IN_AB_PAD = _C_A0 + GDN_W

_Q_FOLD = (MLA_QK ** -0.5) * float(np.log2(np.e))


def _cparams(sem):
    return pltpu.CompilerParams(dimension_semantics=sem, vmem_limit_bytes=VMEM_LIMIT_BYTES)


def _rms(x, w):
    return x * lax.rsqrt(jnp.mean(x * x, axis=-1, keepdims=True) + EPS) * w


def _const_spec(shape):
    nd = len(shape)
    return pl.BlockSpec(shape, lambda *_: (0,) * nd, pipeline_mode=pl.Buffered(1))


def _ffn_chunks():
    tiles = D_FF // MXU_TILE
    first = (tiles + 1) // 2 * MXU_TILE
    return ((0, first), (first, D_FF))


def _swiglu_residual(x, nw_ref, wg_ref, wu_ref, wd_ref):
    h = _rms(x, nw_ref[...]).astype(BF16)
    acc = None
    for lo, hi in _ffn_chunks():
        sl = slice(lo, hi)
        g = jnp.dot(h, wg_ref[:, sl], preferred_element_type=F32)
        u = jnp.dot(h, wu_ref[:, sl], preferred_element_type=F32)
        a = (g * jax.nn.sigmoid(g) * u).astype(BF16)
        d = jnp.dot(a, wd_ref[sl, :], preferred_element_type=F32)
        acc = d if acc is None else acc + d
    return x + 0.5 * acc


def _ffn_body(x_ref, nw_ref, wg_ref, wu_ref, wd_ref, o_ref):
    o_ref[...] = _swiglu_residual(x_ref[...], nw_ref, wg_ref, wu_ref, wd_ref)


def _proj_ffn_body(x_ref, ya_ref, yb_ref, wa_ref, wb_ref, nw_ref, wg_ref, wu_ref, wd_ref, o_ref):
    x = x_ref[...] + _dot(ya_ref[...], wa_ref[...]) + _dot(yb_ref[...], wb_ref[...])
    o_ref[...] = _swiglu_residual(x, nw_ref, wg_ref, wu_ref, wd_ref)


def _ffn(x, nw, wg, wu, wd, *, proj=None, tm=512):
    t = x.shape[0]
    row = lambda i: (i, 0)
    ffn_consts = [nw, wg, wu, wd]
    if proj is None:
        body, acts, consts = _ffn_body, [x], ffn_consts
    else:
        ya, yb, wa, wb = proj
        body, acts, consts = _proj_ffn_body, [x, ya, yb], [wa, wb] + ffn_consts
    return pl.pallas_call(
        body,
        out_shape=jax.ShapeDtypeStruct((t, D_MODEL), F32),
        grid=(t // tm,),
        in_specs=[pl.BlockSpec((tm, a.shape[1]), row) for a in acts] + [_const_spec(c.shape) for c in consts],
        out_specs=pl.BlockSpec((tm, D_MODEL), row),
        compiler_params=_cparams(("arbitrary",)),
        name="ffn" if proj is None else "proj_ffn",
    )(*acts, *consts)


def _rope_body(pos_ref, invf_ref, cos_ref, sin_ref):
    ang = pos_ref[...] * invf_ref[...]
    cos_ref[...] = jnp.cos(ang)
    sin_ref[...] = jnp.sin(ang)


def _rope_tables(positions):
    b, s = positions.shape
    t = b * s
    half = MLA_ROPE // 2
    inv_freq = ROPE_BASE ** (-jnp.arange(0, MLA_ROPE, 2, dtype=F32) / MLA_ROPE)
    per_row = LANES // half
    pos = jnp.repeat(positions.reshape(t).astype(F32), half).reshape(t // per_row, LANES)
    invf = jnp.tile(inv_freq, per_row).reshape(1, LANES)
    rows = t // per_row
    tr = min(rows, 512)
    cos, sin = pl.pallas_call(
        _rope_body,
        out_shape=(jax.ShapeDtypeStruct((rows, LANES), F32),) * 2,
        grid=(rows // tr,),
        in_specs=[pl.BlockSpec((tr, LANES), lambda i: (i, 0)), _const_spec((1, LANES))],
        out_specs=(pl.BlockSpec((tr, LANES), lambda i: (i, 0)),) * 2,
        compiler_params=_cparams(("arbitrary",)),
        name="rope_tables",
    )(pos, invf)
    cos = cos.reshape(t, half)
    sin = sin.reshape(t, half)
    pad = LANES - MLA_QK
    cos_full = jnp.concatenate([jnp.ones((t, MLA_NOPE), F32), cos, cos, jnp.ones((t, pad), F32)], axis=-1)
    sin_full = jnp.concatenate([jnp.zeros((t, MLA_NOPE), F32), sin, sin, jnp.zeros((t, pad), F32)], axis=-1)
    return cos_full, sin_full


def _softplus(x):
    return jnp.maximum(x, 0.0) + jnp.log1p(jnp.exp(-jnp.abs(x)))


def _ab_in_body(x_ref, cos_ref, sin_ref, nw_ref, w1_ref, qn_ref, wq_ref, kvn_ref, wkv_ref, wvt_ref,
                nq_ref, nqp_ref, nk_ref, nkp_ref, convw_ref, alog_ref, dtb_ref,
                q_out, k_out, vt_out, qkv_out, beta_out, g_out, z_out,
                carry_ref, *, tm, tiles_per_seq):
    i = pl.program_id(0)

    @pl.when(i % tiles_per_seq == 0)
    def _():
        carry_ref[0:8, :] = jnp.zeros((8, 3 * GDN_W), F32)

    hw = N_HEADS * HEAD_PAD
    for s in range(tm // AB_SUB):
        rows = slice(s * AB_SUB, (s + 1) * AB_SUB)
        x = x_ref[rows, :]
        hb = _rms(x, nw_ref[...]).astype(BF16)
        p = jnp.dot(hb, w1_ref[...], preferred_element_type=F32)

        c_q = p[:, _C_Q0:_C_KV0]
        c_kv = p[:, _C_KV0:_C_KR0]
        kr = p[:, _C_KR0:_C_KRP0]
        krp = p[:, _C_KRP0:_C_QKV0]
        cqn = _rms(c_q, qn_ref[...]).astype(BF16)
        qq = jnp.dot(cqn, wq_ref[...], preferred_element_type=F32)
        ckvn = _rms(c_kv, kvn_ref[...]).astype(BF16)
        kk = jnp.dot(ckvn, wkv_ref[...], preferred_element_type=F32)
        vt = lax.dot_general(wvt_ref[...], ckvn, (((1,), (1,)), ((), ())), preferred_element_type=F32)
        vrow = lax.broadcasted_iota(jnp.int32, vt.shape, 0) & (VT_ROWS - 1)
        vt_out[0, :, rows] = jnp.where(vrow == MLA_V, 1.0, vt).astype(BF16)
        cos = cos_ref[rows, :]
        sin = sin_ref[rows, :]
        for h in range(N_HEADS):
            sl = slice(h * HEAD_PAD, (h + 1) * HEAD_PAD)
            qh = qq[:, sl]
            qph = qq[:, hw + h * HEAD_PAD: hw + (h + 1) * HEAD_PAD]
            rq = lax.rsqrt(jnp.sum(qh * qh, axis=-1, keepdims=True) * (1.0 / MLA_QK) + EPS)
            qr = (qh * nq_ref[...] * cos + qph * nqp_ref[...] * sin) * (rq * _Q_FOLD)
            q_out[rows, sl] = qr.astype(BF16)
            kh = kk[:, sl] + kr
            rk = lax.rsqrt(jnp.sum(kh * kh, axis=-1, keepdims=True) * (1.0 / MLA_QK) + EPS)
            kro = (kh * nk_ref[...] * cos + krp * nkp_ref[...] * sin) * rk
            k_out[rows, sl] = kro.astype(BF16)

        qkv_pre = p[:, _C_QKV0:_C_Z0]
        base = 8 + s * AB_SUB
        carry_ref[base:base + AB_SUB, :] = qkv_pre
        conv = qkv_pre * convw_ref[GDN_CONV - 1:GDN_CONV, :]
        for j in range(GDN_CONV - 1):
            off = base - (GDN_CONV - 1) + j
            conv = conv + carry_ref[off:off + AB_SUB, :] * convw_ref[j:j + 1, :]
        qkv_out[rows, :] = conv * jax.nn.sigmoid(conv)
        beta_out[rows, :] = jax.nn.sigmoid(p[:, _C_B0:_C_A0])
        g_out[rows, :] = -jnp.exp(alog_ref[...]) * _softplus(p[:, _C_A0:IN_AB_PAD] + dtb_ref[...])
        z_out[rows, :] = p[:, _C_Z0:_C_B0]
    carry_ref[0:8, :] = carry_ref[tm:tm + 8, :]


def _ab_in(x, cos_full, sin_full, wts, *, seq, tm=512):
    t = x.shape[0]
    tiles_per_seq = seq // tm
    row = lambda i: (i, 0)
    hw = N_HEADS * HEAD_PAD
    outs = (
        jax.ShapeDtypeStruct((t, hw), BF16),
        jax.ShapeDtypeStruct((t, hw), BF16),
        jax.ShapeDtypeStruct((t // tm, N_HEADS * VT_ROWS, tm), BF16),
        jax.ShapeDtypeStruct((t, 3 * GDN_W), F32),
        jax.ShapeDtypeStruct((t, GDN_W), F32),
        jax.ShapeDtypeStruct((t, GDN_W), F32),
        jax.ShapeDtypeStruct((t, GDN_W), F32),
    )
    consts = [wts["nw"], wts["w1"], wts["q_norm"], wts["wq"], wts["kv_norm"], wts["wkv"], wts["wvt"],
              wts["nq"], wts["nqp"], wts["nk"], wts["nkp"], wts["conv_w"], wts["a_log"], wts["dt_bias"]]
    out_specs = tuple(
        pl.BlockSpec((1,) + o.shape[1:], lambda i: (i, 0, 0)) if len(o.shape) == 3
        else pl.BlockSpec((tm, o.shape[1]), row) for o in outs)
    return pl.pallas_call(
        functools.partial(_ab_in_body, tm=tm, tiles_per_seq=tiles_per_seq),
        out_shape=outs,
        grid=(t // tm,),
        in_specs=[pl.BlockSpec((tm, D_MODEL), row), pl.BlockSpec((tm, LANES), row),
                  pl.BlockSpec((tm, LANES), row)] + [_const_spec(c.shape) for c in consts],
        out_specs=out_specs,
        scratch_shapes=[pltpu.VMEM((tm + 8, 3 * GDN_W), F32)],
        compiler_params=_cparams(("arbitrary",)),
        name="ab_in",
    )(x, cos_full, sin_full, *consts)


def _attn_body(q_ref, k_ref, vt_ref, o_ref, m_ref, acc_ref, *, tq, tk, vtile, nh):
    qi = pl.program_id(2)
    nt = (((1,), (1,)), ((), ()))
    sub = tk // vtile
    m_ref[...] = jnp.full(m_ref.shape, -jnp.inf, F32)
    acc_ref[...] = jnp.zeros(acc_ref.shape, F32)

    def block(kb, masked):
        row0 = pl.multiple_of(kb * tk, tk)
        st, pt, alpha = [], [], []
        for e in range(nh):
            q = q_ref[:, e * HEAD_PAD:(e + 1) * HEAD_PAD]
            k = k_ref[pl.ds(row0, tk), e * HEAD_PAD:(e + 1) * HEAD_PAD]
            s = lax.dot_general(k, q, nt, preferred_element_type=F32)
            if masked:
                key = kb * tk + lax.broadcasted_iota(jnp.int32, (tk, tq), 0)
                qry = qi * tq + lax.broadcasted_iota(jnp.int32, (tk, tq), 1)
                s = jnp.where(key <= qry, s, -jnp.inf)
            st.append(s)
        for e in range(nh):
            m_prev = m_ref[e]
            m_new = jnp.maximum(m_prev, jnp.max(st[e], axis=0, keepdims=True))
            alpha.append(jnp.exp2(m_prev - m_new))
            pt.append(jnp.exp2(st[e] - m_new).astype(BF16))
            m_ref[e] = m_new
        for e in range(nh):
            pv = None
            for j in range(sub):
                vt = vt_ref[kb * sub + j, e * VT_ROWS:(e + 1) * VT_ROWS, :]
                d = _dot(vt, pt[e][j * vtile:(j + 1) * vtile, :])
                pv = d if pv is None else pv + d
            acc_ref[e] = alpha[e] * acc_ref[e] + pv

    def body(kb, carry):
        block(kb, False)
        return carry

    nfull = (qi * tq) // tk
    lax.fori_loop(0, nfull, body, 0)
    for j in range(max(tq // tk, 1)):
        block(nfull + j, True)
    for pr in range(nh // 2):
        halves = []
        for e in (2 * pr, 2 * pr + 1):
            acc = acc_ref[e]
            halves.append(acc[:MLA_V] * (1.0 / acc[MLA_V:MLA_V + 1]))
        o_ref[:, pr * 2 * MLA_V:(pr + 1) * 2 * MLA_V] = jnp.concatenate(halves, axis=0).T.astype(o_ref.dtype)


def _attention(q, k, vt, *, batch, seq, tq=512, tk=512, nh=4):
    t = q.shape[0]
    vtile = vt.shape[2]
    assert tk % vtile == 0 and (tq % tk == 0 or tk % tq == 0) and seq % tk == 0 and seq % tq == 0
    nq = seq // tq
    nv = seq // vtile
    return pl.pallas_call(
        functools.partial(_attn_body, tq=tq, tk=tk, vtile=vtile, nh=nh),
        out_shape=jax.ShapeDtypeStruct((t, N_HEADS * MLA_V), BF16),
        grid=(batch, N_HEADS // nh, nq),
        in_specs=[
            pl.BlockSpec((tq, nh * HEAD_PAD), lambda b, p, i: (b * nq + i, p)),
            pl.BlockSpec((seq, nh * HEAD_PAD), lambda b, p, i: (b, p)),
            pl.BlockSpec((nv, nh * VT_ROWS, vtile), lambda b, p, i: (b, p, 0)),
        ],
        out_specs=pl.BlockSpec((tq, nh * MLA_V), lambda b, p, i: (b * nq + i, p)),
        scratch_shapes=[pltpu.VMEM((nh, 1, tq), F32), pltpu.VMEM((nh, VT_ROWS, tq), F32)],
        compiler_params=_cparams(("arbitrary", "arbitrary", "arbitrary")),
        name="mla_attention",
    )(q, k, vt)


def _split3(a):
    hi = a.astype(BF16)
    r1 = a - hi.astype(F32)
    mid = r1.astype(BF16)
    lo = (r1 - mid.astype(F32)).astype(BF16)
    return hi, mid, lo


def _split2(a):
    hi = a.astype(BF16)
    lo = (a - hi.astype(F32)).astype(BF16)
    return hi, lo


def _dot(a, b):
    return jnp.dot(a, b, preferred_element_type=F32)


def _dot_exact_lhs(a_bf16, b):
    hi, mid, lo = _split3(b)
    return _dot(a_bf16, hi) + _dot(a_bf16, mid) + _dot(a_bf16, lo)


def _gdn_body(qkv_ref, beta_ref, g_ref, z_ref, onorm_ref, o_ref, state_ref, *, tc):
    c = GDN_CHUNK
    npairs = N_HEADS // 2

    @pl.when(pl.program_id(1) == 0)
    def _():
        state_ref[...] = jnp.zeros(state_ref.shape, F32)

    ri = lax.broadcasted_iota(jnp.int32, (c, LANES), 0)
    li = lax.broadcasted_iota(jnp.int32, (c, LANES), 1)
    lj = li & (c - 1)
    tril2 = lj <= ri
    strict2 = lj < ri
    eye2 = (lj == ri).astype(F32)
    r2 = lax.broadcasted_iota(jnp.int32, (2 * c, LANES), 0)
    l2 = lax.broadcasted_iota(jnp.int32, (2 * c, LANES), 1)
    blkmask = (r2 < c) == (l2 < c)
    gsum_m = blkmask.astype(BF16)
    rr = lax.broadcasted_iota(jnp.int32, (c, c), 0)
    cc = lax.broadcasted_iota(jnp.int32, (c, c), 1)
    ltri = (cc <= rr).astype(BF16)
    first_half = li < c

    def blk(rp):
        return jnp.where(blkmask, jnp.concatenate([rp, rp], axis=0), 0.0)

    def mm(lp, rp):
        return _dot(lp.astype(BF16), blk(rp).astype(BF16))

    mmi = mm

    def gsum(y):
        hi, lo = _split2(y)
        return _dot(hi, gsum_m) + _dot(lo, gsum_m)

    nt = (((1,), (1,)), ((), ()))
    tn = (((0,), (0,)), ((), ()))
    level_masks = []
    for sh in (3, 4, 5):
        same = (ri >> (sh + 1)) == (lj >> (sh + 1))
        level_masks.append(same & (((ri >> sh) & 1) == 1) & (((lj >> sh) & 1) == 0))
    diag8 = ((ri >> 3) == (lj >> 3)) & strict2

    nch = tc // c
    probs = [(ch, p) for ch in range(nch) for p in range(npairs)]
    rows = lambda ch: slice(ch * c, (ch + 1) * c)
    ls = lambda p: slice(p * LANES, (p + 1) * LANES)
    qn, kn, kb, vb, gcum, decay, egc = {}, {}, {}, {}, {}, {}, {}
    for pr in probs:
        ch, p = pr
        q = qkv_ref[rows(ch), p * LANES:(p + 1) * LANES]
        k = qkv_ref[rows(ch), GDN_W + p * LANES:GDN_W + (p + 1) * LANES]
        v = qkv_ref[rows(ch), 2 * GDN_W + p * LANES:2 * GDN_W + (p + 1) * LANES]
        bt = beta_ref[rows(ch), ls(p)]
        gg = g_ref[rows(ch), ls(p)]
        qn[pr] = q * lax.rsqrt(gsum(q * q) + EPS) * (GDN_DK ** -0.5)
        kn[pr] = k * lax.rsqrt(gsum(k * k) + EPS)
        cs = _dot_exact_lhs(ltri, jnp.concatenate([gg, jnp.where(strict2, gg, 0.0)], axis=1))
        gcum[pr] = cs[:, :LANES]
        decay[pr] = jnp.exp(jnp.where(tril2, cs[:, LANES:], -jnp.inf))
        egc[pr] = jnp.exp(gcum[pr])
        kb[pr] = kn[pr] * bt
        vb[pr] = v * bt
    n, attn = {}, {}
    for pr in probs:
        lhs = jnp.concatenate([kb[pr], qn[pr]], axis=0).astype(BF16)
        res = lax.dot_general(lhs, blk(kn[pr]).astype(BF16), nt, preferred_element_type=F32)
        n[pr] = jnp.where(strict2, res[:c] * decay[pr], 0.0)
        attn[pr] = res[c:] * decay[pr]
    nd = {pr: jnp.where(diag8, n[pr], 0.0) for pr in probs}
    n2 = {pr: mmi(nd[pr], nd[pr]) for pr in probs}
    n4 = {pr: mmi(n2[pr], n2[pr]) for pr in probs}
    tinv = {pr: mmi(eye2 - nd[pr], eye2 + n2[pr]) for pr in probs}
    tinv = {pr: mmi(tinv[pr], eye2 + n4[pr]) for pr in probs}
    for lm in level_masks:
        xk = {pr: mmi(tinv[pr], jnp.where(lm, n[pr], 0.0)) for pr in probs}
        tinv = {pr: tinv[pr] - mmi(xk[pr], tinv[pr]) for pr in probs}
    u, w = {}, {}
    for pr in probs:
        rhs = jnp.concatenate([blk(vb[pr]), blk(kb[pr] * egc[pr])], axis=1).astype(BF16)
        uw = _dot(tinv[pr].astype(BF16), rhs)
        u[pr] = uw[:, :LANES]
        w[pr] = uw[:, LANES:]
    st = [state_ref[p] for p in range(npairs)]
    for ch in range(nch):
        for p in range(npairs):
            pr = (ch, p)
            glast = gcum[pr][c - 1:c, :]
            ktail = kn[pr] * jnp.exp(glast - gcum[pr])
            wq = jnp.concatenate([w[pr], qn[pr] * egc[pr]], axis=0).astype(BF16)
            ws = _dot(wq, blk(st[p]).astype(BF16))
            vnew = u[pr] - ws[:c]
            o = ws[c:] + mm(attn[pr], vnew)
            upd = lax.dot_general(ktail.astype(BF16), vnew.astype(BF16), tn, preferred_element_type=F32)
            st[p] = st[p] * jnp.exp(glast) + jnp.where(first_half, upd[0:c, :], upd[c:2 * c, :])
            ms = gsum(o * o) * (1.0 / GDN_DV)
            zz = z_ref[rows(ch), ls(p)]
            y = o * lax.rsqrt(ms + EPS) * onorm_ref[...] * (zz * jax.nn.sigmoid(zz))
            o_ref[rows(ch), ls(p)] = y.astype(o_ref.dtype)
    for p in range(npairs):
        state_ref[p] = st[p]


def _gdn(qkv, beta, g, z, onorm2, *, batch, seq, tc=512):
    t = qkv.shape[0]
    nb = seq // tc
    row = lambda b, i: (b * nb + i, 0)
    return pl.pallas_call(
        functools.partial(_gdn_body, tc=tc),
        out_shape=jax.ShapeDtypeStruct((t, GDN_W), BF16),
        grid=(batch, nb),
        in_specs=[pl.BlockSpec((tc, 3 * GDN_W), row), pl.BlockSpec((tc, GDN_W), row),
                  pl.BlockSpec((tc, GDN_W), row), pl.BlockSpec((tc, GDN_W), row),
                  _const_spec((1, LANES))],
        out_specs=pl.BlockSpec((tc, GDN_W), row),
        scratch_shapes=[pltpu.VMEM((N_HEADS // 2, GDN_DK, LANES), F32)],
        compiler_params=_cparams(("arbitrary", "arbitrary")),
        name="gated_delta_rule",
    )(qkv, beta, g, z, onorm2)


def _sg_body(x_ref, nw_ref, win_ref, vn_ref, ws_ref, bs_ref, wout_ref, o_ref, gate_ref, *, tm):
    x = x_ref[...]
    hb = _rms(x, nw_ref[...]).astype(BF16)
    uv = _dot(hb, win_ref[...])
    uv = 0.5 * uv * (1.0 + lax.erf(uv * (2.0 ** -0.5)))
    rr = lax.broadcasted_iota(jnp.int32, (SG_CHUNK, SG_CHUNK), 0)
    cc = lax.broadcasted_iota(jnp.int32, (SG_CHUNK, SG_CHUNK), 1)
    causal = cc <= rr
    for g in range(SG_GROUPS):
        cs = slice(SG_WIDTH + g * SG_CG, SG_WIDTH + (g + 1) * SG_CG)
        vg = uv[:, cs]
        vg = vg * lax.rsqrt(jnp.mean(vg * vg, axis=-1, keepdims=True) + EPS) * vn_ref[:, g * SG_CG:(g + 1) * SG_CG]
        vg = vg.astype(BF16)
        wc = jnp.where(causal, ws_ref[g], 0.0).astype(BF16)
        bias = bs_ref[g]
        for n in range(tm // SG_CHUNK):
            rs = slice(n * SG_CHUNK, (n + 1) * SG_CHUNK)
            gate_ref[rs, g * SG_CG:(g + 1) * SG_CG] = _dot(wc, vg[rs, :]) + bias
    y = (uv[:, :SG_WIDTH] * gate_ref[...]).astype(BF16)
    o_ref[...] = x + _dot(y, wout_ref[...])


def _sg_mixer(x, nw, win, vn, ws, bs, wout, *, tm=256):
    t = x.shape[0]
    row = lambda i: (i, 0)
    consts = [nw, win, vn, ws, bs, wout]
    return pl.pallas_call(
        functools.partial(_sg_body, tm=tm),
        out_shape=jax.ShapeDtypeStruct((t, D_MODEL), F32),
        grid=(t // tm,),
        in_specs=[pl.BlockSpec((tm, D_MODEL), row)] + [_const_spec(c.shape) for c in consts],
        out_specs=pl.BlockSpec((tm, D_MODEL), row),
        scratch_shapes=[pltpu.VMEM((tm, SG_WIDTH), F32)],
        compiler_params=_cparams(("arbitrary",)),
        name="spatial_gating",
    )(x, *consts)


def _pad_heads(w, width, pad_to):
    k = w.shape[0]
    w = w.reshape(k, N_HEADS, width)
    return jnp.pad(w, ((0, 0), (0, 0), (0, pad_to - width))).reshape(k, N_HEADS * pad_to)


def _rot_partner(w_rope):
    half = MLA_ROPE // 2
    return jnp.concatenate([-w_rope[..., half:], w_rope[..., :half]], axis=-1)


def _place_rope(w_rope):
    return jnp.pad(w_rope, ((0, 0), (MLA_NOPE, LANES - MLA_QK)))


def _even_weights(l, i, norm_w, ab_w_in, mla_q_norm, mla_w_q_b, mla_kv_norm, mla_w_kv_b, mla_qk_norm_q,
                  mla_qk_norm_k, gdn_conv_w, gdn_a_log, gdn_dt_bias, gdn_out_norm, ab_w_out):
    w_in = ab_w_in[i]
    o_kr = MLA_Q_LORA + MLA_KV_LORA
    o_qkv = o_kr + MLA_ROPE
    o_z = o_qkv + 3 * GDN_W
    o_b = o_z + GDN_W
    o_a = o_b + N_HEADS
    w_kr = w_in[:, o_kr:o_qkv]
    w1 = jnp.concatenate([
        w_in[:, :o_kr],
        _place_rope(w_kr),
        _place_rope(_rot_partner(w_kr)),
        w_in[:, o_qkv:o_b],
        jnp.repeat(w_in[:, o_b:o_a], GDN_DK, axis=1),
        jnp.repeat(w_in[:, o_a:o_a + N_HEADS], GDN_DK, axis=1),
    ], axis=1).astype(BF16)
    wq = mla_w_q_b[i].reshape(MLA_Q_LORA, N_HEADS, MLA_QK)
    wq_part = jnp.concatenate([jnp.zeros_like(wq[..., :MLA_NOPE]), _rot_partner(wq[..., MLA_NOPE:])], axis=-1)
    wq2 = jnp.concatenate([
        _pad_heads(wq.reshape(MLA_Q_LORA, -1), MLA_QK, HEAD_PAD),
        _pad_heads(wq_part.reshape(MLA_Q_LORA, -1), MLA_QK, HEAD_PAD),
    ], axis=1).astype(BF16)
    wkv = mla_w_kv_b[i].reshape(MLA_KV_LORA, N_HEADS, MLA_NOPE + MLA_V)
    wkv2 = _pad_heads(wkv[..., :MLA_NOPE].reshape(MLA_KV_LORA, -1), MLA_NOPE, HEAD_PAD).astype(BF16)
    wvt = _pad_heads(wkv[..., MLA_NOPE:].reshape(MLA_KV_LORA, -1), MLA_V, VT_ROWS).T.astype(BF16)

    def norm_pair(nvec):
        full = jnp.pad(nvec, (0, LANES - MLA_QK)).reshape(1, LANES)
        rope = nvec[MLA_NOPE:]
        half = MLA_ROPE // 2
        part = jnp.concatenate([rope[half:], rope[:half]])
        part = jnp.pad(part, (MLA_NOPE, LANES - MLA_QK)).reshape(1, LANES)
        return full, part

    nq, nqp = norm_pair(mla_qk_norm_q[i])
    nk, nkp = norm_pair(mla_qk_norm_k[i])
    w_out = ab_w_out[i].astype(BF16)
    return dict(
        nw=norm_w[l, 1].reshape(1, D_MODEL), w1=w1, q_norm=mla_q_norm[i].reshape(1, -1), wq=wq2,
        kv_norm=mla_kv_norm[i].reshape(1, -1), wkv=wkv2, wvt=wvt, nq=nq, nqp=nqp, nk=nk, nkp=nkp,
        conv_w=gdn_conv_w[i], a_log=jnp.repeat(gdn_a_log[i], GDN_DK).reshape(1, GDN_W),
        dt_bias=jnp.repeat(gdn_dt_bias[i], GDN_DK).reshape(1, GDN_W),
        onorm2=jnp.tile(gdn_out_norm[i], 2).reshape(1, LANES),
        w_out_a=w_out[:N_HEADS * MLA_V], w_out_b=w_out[N_HEADS * MLA_V:],
    )


def kernel(x, positions, norm_w, ffn_w_gate, ffn_w_up, ffn_w_down, ab_w_in, mla_q_norm, mla_w_q_b, mla_kv_norm, mla_w_kv_b, mla_qk_norm_q, mla_qk_norm_k, gdn_conv_w, gdn_a_log, gdn_dt_bias, gdn_out_norm, ab_w_out, sg_w_in, sg_v_norm, sg_w_s, sg_b_s, sg_w_out):
    batch, seq, _ = x.shape
    depth = norm_w.shape[0]
    t = batch * seq
    xf = x.reshape(t, D_MODEL)
    cos_full, sin_full = _rope_tables(positions)

    def ffn(xf, l, j, slot, proj=None):
        return _ffn(xf, norm_w[l, slot].reshape(1, D_MODEL), ffn_w_gate[l, j].astype(BF16),
                    ffn_w_up[l, j].astype(BF16), ffn_w_down[l, j].astype(BF16), proj=proj)

    for l in range(depth):
        i = l // 2
        xf = ffn(xf, l, 0, 0)
        proj = None
        if l % 2 == 0:
            wts = _even_weights(l, i, norm_w, ab_w_in, mla_q_norm, mla_w_q_b, mla_kv_norm, mla_w_kv_b,
                                mla_qk_norm_q, mla_qk_norm_k, gdn_conv_w, gdn_a_log, gdn_dt_bias,
                                gdn_out_norm, ab_w_out)
            q, k, vt, qkv, beta, g, z = _ab_in(xf, cos_full, sin_full, wts, seq=seq)
            y_mla = _attention(q, k, vt, batch=batch, seq=seq)
            y_gdn = _gdn(qkv, beta, g, z, wts["onorm2"], batch=batch, seq=seq)
            proj = (y_mla, y_gdn, wts["w_out_a"], wts["w_out_b"])
        else:
            bs = jnp.broadcast_to(sg_b_s[i][:, :, None], (SG_GROUPS, SG_CHUNK, SG_CG))
            xf = _sg_mixer(xf, norm_w[l, 1].reshape(1, D_MODEL), sg_w_in[i].astype(BF16),
                           sg_v_norm[i].reshape(1, SG_WIDTH), sg_w_s[i], bs, sg_w_out[i].astype(BF16))
        xf = ffn(xf, l, 1, 2, proj=proj)
    return xf.reshape(batch, seq, D_MODEL)
```

```python
import functools

import numpy as np
import jax
import jax.numpy as jnp
from jax import lax
from jax.experimental import pallas as pl
from jax.experimental.pallas import tpu as pltpu

F32 = jnp.float32
BF16 = jnp.bfloat16

D_MODEL = 1024
D_FF = 2816
EPS = 1e-6
N_HEADS = 8
MLA_Q_LORA = 256
MLA_KV_LORA = 128
MLA_NOPE = 64
MLA_ROPE = 32
MLA_V = 64
MLA_QK = MLA_NOPE + MLA_ROPE
ROPE_BASE = 10000.0
GDN_DK = 64
GDN_DV = 64
GDN_CONV = 4
GDN_CHUNK = 64
GDN_W = N_HEADS * GDN_DK
SG_GROUPS = 8
SG_WIDTH = 2048
SG_CHUNK = 128
SG_CG = SG_WIDTH // SG_GROUPS

LANES = 128
MXU_TILE = 256
VMEM_LIMIT_BYTES = 56 * 1024 * 1024
HEAD_PAD = LANES
VT_ROWS = 128
SHIFT_LANE = MLA_QK
FIXED_REF_MAX_BOUND = 60.0
SG_SUB = 256
AB_SUB = 256
_C_Q0 = 0
_C_KV0 = _C_Q0 + MLA_Q_LORA
_C_KR0 = _C_KV0 + MLA_KV_LORA
_C_KRP0 = _C_KR0 + LANES
_C_QKV0 = _C_KRP0 + LANES
_C_Z0 = _C_QKV0 + 3 * GDN_W
_C_B0 = _C_Z0 + GDN_W
_C_A0 = _C_B0 + GDN_W
IN_AB_PAD = _C_A0 + GDN_W

_Q_FOLD = (MLA_QK ** -0.5) * float(np.log2(np.e))


def _cparams(sem):
    return pltpu.CompilerParams(dimension_semantics=sem, vmem_limit_bytes=VMEM_LIMIT_BYTES)


def _rms(x, w):
    return x * lax.rsqrt(jnp.mean(x * x, axis=-1, keepdims=True) + EPS) * w


def _const_spec(shape):
    nd = len(shape)
    return pl.BlockSpec(shape, lambda *_: (0,) * nd, pipeline_mode=pl.Buffered(1))


def _ffn_chunks():
    tiles = D_FF // MXU_TILE
    first = (tiles + 1) // 2 * MXU_TILE
    return ((0, first), (first, D_FF))


def _swiglu_residual(x, nw_ref, wg_ref, wu_ref, wd_ref):
    h = _rms(x, nw_ref[...]).astype(BF16)
    acc = None
    for lo, hi in _ffn_chunks():
        sl = slice(lo, hi)
        g = jnp.dot(h, wg_ref[:, sl], preferred_element_type=F32)
        u = jnp.dot(h, wu_ref[:, sl], preferred_element_type=F32)
        a = (g * jax.nn.sigmoid(g) * u).astype(BF16)
        d = jnp.dot(a, wd_ref[sl, :], preferred_element_type=F32)
        acc = d if acc is None else acc + d
    return x + 0.5 * acc


def _ffn_body(x_ref, nw_ref, wg_ref, wu_ref, wd_ref, o_ref):
    o_ref[...] = _swiglu_residual(x_ref[...], nw_ref, wg_ref, wu_ref, wd_ref)


def _proj_ffn_body(x_ref, ya_ref, yb_ref, wa_ref, wb_ref, nw_ref, wg_ref, wu_ref, wd_ref, o_ref):
    x = x_ref[...] + _dot(ya_ref[...], wa_ref[...]) + _dot(yb_ref[...], wb_ref[...])
    o_ref[...] = _swiglu_residual(x, nw_ref, wg_ref, wu_ref, wd_ref)


def _ffn(x, nw, wg, wu, wd, l, j, *, proj=None, tm=512):
    t = x.shape[0]
    row = lambda i: (i, 0)

    def stacked_spec(w):
        return pl.BlockSpec((None, None) + w.shape[2:], lambda i: (l, j, 0, 0), pipeline_mode=pl.Buffered(1))

    ffn_consts = [nw, wg, wu, wd]
    ffn_specs = [_const_spec(nw.shape), stacked_spec(wg), stacked_spec(wu), stacked_spec(wd)]
    if proj is None:
        body, acts, consts, cspecs = _ffn_body, [x], ffn_consts, ffn_specs
    else:
        ya, yb, wa, wb = proj
        body, acts, consts = _proj_ffn_body, [x, ya, yb], [wa, wb] + ffn_consts
        cspecs = [_const_spec(wa.shape), _const_spec(wb.shape)] + ffn_specs
    return pl.pallas_call(
        body,
        out_shape=jax.ShapeDtypeStruct((t, D_MODEL), F32),
        grid=(t // tm,),
        in_specs=[pl.BlockSpec((tm, a.shape[1]), row) for a in acts] + cspecs,
        out_specs=pl.BlockSpec((tm, D_MODEL), row),
        compiler_params=_cparams(("arbitrary",)),
        name="ffn" if proj is None else "proj_ffn",
    )(*acts, *consts)


def _rope_body(pos_ref, invf_ref, cos_ref, sin_ref):
    ang = pos_ref[...] * invf_ref[...]
    cos_ref[...] = jnp.cos(ang)
    sin_ref[...] = jnp.sin(ang)


def _rope_tables(positions):
    b, s = positions.shape
    t = b * s
    half = MLA_ROPE // 2
    inv_freq = ROPE_BASE ** (-jnp.arange(0, MLA_ROPE, 2, dtype=F32) / MLA_ROPE)
    per_row = LANES // half
    pos = jnp.repeat(positions.reshape(t).astype(F32), half).reshape(t // per_row, LANES)
    invf = jnp.tile(inv_freq, per_row).reshape(1, LANES)
    rows = t // per_row
    tr = min(rows, 512)
    cos, sin = pl.pallas_call(
        _rope_body,
        out_shape=(jax.ShapeDtypeStruct((rows, LANES), F32),) * 2,
        grid=(rows // tr,),
        in_specs=[pl.BlockSpec((tr, LANES), lambda i: (i, 0)), _const_spec((1, LANES))],
        out_specs=(pl.BlockSpec((tr, LANES), lambda i: (i, 0)),) * 2,
        compiler_params=_cparams(("arbitrary",)),
        name="rope_tables",
    )(pos, invf)
    cos = cos.reshape(t, half)
    sin = sin.reshape(t, half)
    pad = LANES - MLA_QK
    cos_full = jnp.concatenate([jnp.ones((t, MLA_NOPE), F32), cos, cos, jnp.ones((t, pad), F32)], axis=-1)
    sin_full = jnp.concatenate([jnp.zeros((t, MLA_NOPE), F32), sin, sin, jnp.zeros((t, pad), F32)], axis=-1)
    return cos_full, sin_full


def _softplus(x):
    return jnp.maximum(x, 0.0) + jnp.log1p(jnp.exp(-jnp.abs(x)))


def _ab_in_body(x_ref, cos_ref, sin_ref, nw_ref, w1_ref, qn_ref, wq_ref, kvn_ref, wkv_ref, wvt_ref,
                nq_ref, nqp_ref, nk_ref, nkp_ref, qshift_ref, kone_ref, convw_ref, alog_ref, dtb_ref,
                q_out, k_out, vt_out, qkv_out, beta_out, g_out, z_out,
                carry_ref, *, tm, tiles_per_seq):
    i = pl.program_id(0)

    @pl.when(i % tiles_per_seq == 0)
    def _():
        carry_ref[0:8, :] = jnp.zeros((8, 3 * GDN_W), F32)

    hw = N_HEADS * HEAD_PAD
    for s in range(tm // AB_SUB):
        rows = slice(s * AB_SUB, (s + 1) * AB_SUB)
        x = x_ref[rows, :]
        hb = _rms(x, nw_ref[...]).astype(BF16)
        p = jnp.dot(hb, w1_ref[...], preferred_element_type=F32)

        c_q = p[:, _C_Q0:_C_KV0]
        c_kv = p[:, _C_KV0:_C_KR0]
        kr = p[:, _C_KR0:_C_KRP0]
        krp = p[:, _C_KRP0:_C_QKV0]
        cqn = _rms(c_q, qn_ref[...]).astype(BF16)
        qq = jnp.dot(cqn, wq_ref[...], preferred_element_type=F32)
        ckvn = _rms(c_kv, kvn_ref[...]).astype(BF16)
        kk = jnp.dot(ckvn, wkv_ref[...], preferred_element_type=F32)
        vt = lax.dot_general(wvt_ref[...], ckvn, (((1,), (1,)), ((), ())), preferred_element_type=F32)
        vrow = lax.broadcasted_iota(jnp.int32, vt.shape, 0) & (VT_ROWS - 1)
        vt_out[0, :, rows] = jnp.where(vrow == MLA_V, 1.0, vt).astype(BF16)
        cos = cos_ref[rows, :]
        sin = sin_ref[rows, :]
        for h in range(N_HEADS):
            sl = slice(h * HEAD_PAD, (h + 1) * HEAD_PAD)
            qh = qq[:, sl]
            qph = qq[:, hw + h * HEAD_PAD: hw + (h + 1) * HEAD_PAD]
            rq = lax.rsqrt(jnp.sum(qh * qh, axis=-1, keepdims=True) * (1.0 / MLA_QK) + EPS)
            qr = (qh * nq_ref[...] * cos + qph * nqp_ref[...] * sin) * (rq * _Q_FOLD)
            q_out[rows, sl] = (qr + qshift_ref[...]).astype(BF16)
            kh = kk[:, sl] + kr
            rk = lax.rsqrt(jnp.sum(kh * kh, axis=-1, keepdims=True) * (1.0 / MLA_QK) + EPS)
            kro = (kh * nk_ref[...] * cos + krp * nkp_ref[...] * sin) * rk
            k_out[rows, sl] = (kro + kone_ref[...]).astype(BF16)

        qkv_pre = p[:, _C_QKV0:_C_Z0]
        base = 8 + s * AB_SUB
        carry_ref[base:base + AB_SUB, :] = qkv_pre
        conv = qkv_pre * convw_ref[GDN_CONV - 1:GDN_CONV, :]
        for j in range(GDN_CONV - 1):
            off = base - (GDN_CONV - 1) + j
            conv = conv + carry_ref[off:off + AB_SUB, :] * convw_ref[j:j + 1, :]
        qkv_out[rows, :] = conv * jax.nn.sigmoid(conv)
        beta_out[rows, :] = jax.nn.sigmoid(p[:, _C_B0:_C_A0])
        g_out[rows, :] = -jnp.exp(alog_ref[...]) * _softplus(p[:, _C_A0:IN_AB_PAD] + dtb_ref[...])
        z_out[rows, :] = p[:, _C_Z0:_C_B0]
    carry_ref[0:8, :] = carry_ref[tm:tm + 8, :]


def _ab_in(x, cos_full, sin_full, wts, *, seq, tm=512):
    t = x.shape[0]
    tiles_per_seq = seq // tm
    row = lambda i: (i, 0)
    hw = N_HEADS * HEAD_PAD
    outs = (
        jax.ShapeDtypeStruct((t, hw), BF16),
        jax.ShapeDtypeStruct((t, hw), BF16),
        jax.ShapeDtypeStruct((t // tm, N_HEADS * VT_ROWS, tm), BF16),
        jax.ShapeDtypeStruct((t, 3 * GDN_W), F32),
        jax.ShapeDtypeStruct((t, GDN_W), F32),
        jax.ShapeDtypeStruct((t, GDN_W), F32),
        jax.ShapeDtypeStruct((t, GDN_W), F32),
    )
    consts = [wts["nw"], wts["w1"], wts["q_norm"], wts["wq"], wts["kv_norm"], wts["wkv"], wts["wvt"],
              wts["nq"], wts["nqp"], wts["nk"], wts["nkp"], wts["qshift"], wts["kone"], wts["conv_w"], wts["a_log"], wts["dt_bias"]]
    out_specs = tuple(
        pl.BlockSpec((1,) + o.shape[1:], lambda i: (i, 0, 0)) if len(o.shape) == 3
        else pl.BlockSpec((tm, o.shape[1]), row) for o in outs)
    return pl.pallas_call(
        functools.partial(_ab_in_body, tm=tm, tiles_per_seq=tiles_per_seq),
        out_shape=outs,
        grid=(t // tm,),
        in_specs=[pl.BlockSpec((tm, D_MODEL), row), pl.BlockSpec((tm, LANES), row),
                  pl.BlockSpec((tm, LANES), row)] + [_const_spec(c.shape) for c in consts],
        out_specs=out_specs,
        scratch_shapes=[pltpu.VMEM((tm + 8, 3 * GDN_W), F32)],
        compiler_params=_cparams(("arbitrary",)),
        name="ab_in",
    )(x, cos_full, sin_full, *consts)


def _attn_body(q_ref, k_ref, vt_ref, o_ref, m_ref, acc_ref, *, tq, tk, vtile, nh, online):
    qi = pl.program_id(2)
    nt = (((1,), (1,)), ((), ()))
    sub = tk // vtile
    if online:
        m_ref[...] = jnp.full(m_ref.shape, -jnp.inf, F32)
    acc_ref[...] = jnp.zeros(acc_ref.shape, F32)

    def block(kb, masked):
        row0 = pl.multiple_of(kb * tk, tk)
        def scores(e):
            q = q_ref[:, e * HEAD_PAD:(e + 1) * HEAD_PAD]
            k = k_ref[pl.ds(row0, tk), e * HEAD_PAD:(e + 1) * HEAD_PAD]
            s = lax.dot_general(k, q, nt, preferred_element_type=F32)
            if masked:
                key = kb * tk + lax.broadcasted_iota(jnp.int32, (tk, tq), 0)
                qry = qi * tq + lax.broadcasted_iota(jnp.int32, (tk, tq), 1)
                s = jnp.where(key <= qry, s, -jnp.inf)
            return s

        st = [scores(e) for e in range(nh)]
        if online:
            m_new = [jnp.maximum(m_ref[e], jnp.max(st[e], axis=0, keepdims=True)) for e in range(nh)]
            alpha = [jnp.exp2(m_ref[e] - m_new[e]) for e in range(nh)]
            pt = [jnp.exp2(st[e] - m_new[e]).astype(BF16) for e in range(nh)]
        else:
            pt = [jnp.exp2(st[e]).astype(BF16) for e in range(nh)]
        for e in range(nh):
            pv = None
            for j in range(sub):
                vt = vt_ref[kb * sub + j, e * VT_ROWS:(e + 1) * VT_ROWS, :]
                d = _dot(vt, pt[e][j * vtile:(j + 1) * vtile, :])
                pv = d if pv is None else pv + d
            if online:
                m_ref[e] = m_new[e]
                acc_ref[e] = alpha[e] * acc_ref[e] + pv
            else:
                acc_ref[e] = acc_ref[e] + pv

    def body(kb, carry):
        block(kb, False)
        return carry

    nfull = (qi * tq) // tk
    lax.fori_loop(0, nfull, body, 0)
    for j in range(max(tq // tk, 1)):
        block(nfull + j, True)
    for pr in range(nh // 2):
        halves = []
        for e in (2 * pr, 2 * pr + 1):
            acc = acc_ref[e]
            halves.append(acc[:MLA_V] * (1.0 / acc[MLA_V:MLA_V + 1]))
        o_ref[:, pr * 2 * MLA_V:(pr + 1) * 2 * MLA_V] = jnp.concatenate(halves, axis=0).T.astype(o_ref.dtype)


def _attention(q, k, vt, logit_bound, *, batch, seq, tq=512, tk=512, nh=4):
    t = q.shape[0]
    vtile = vt.shape[2]
    assert tk % vtile == 0 and (tq % tk == 0 or tk % tq == 0) and seq % tk == 0 and seq % tq == 0
    nq = seq // tq
    nv = seq // vtile

    def run(online):
        return pl.pallas_call(
            functools.partial(_attn_body, tq=tq, tk=tk, vtile=vtile, nh=nh, online=online),
            out_shape=jax.ShapeDtypeStruct((t, N_HEADS * MLA_V), BF16),
            grid=(batch, N_HEADS // nh, nq),
            in_specs=[
                pl.BlockSpec((tq, nh * HEAD_PAD), lambda b, p, i: (b * nq + i, p)),
                pl.BlockSpec((seq, nh * HEAD_PAD), lambda b, p, i: (b, p)),
                pl.BlockSpec((nv, nh * VT_ROWS, vtile), lambda b, p, i: (b, p, 0)),
            ],
            out_specs=pl.BlockSpec((tq, nh * MLA_V), lambda b, p, i: (b * nq + i, p)),
            scratch_shapes=[pltpu.VMEM((nh, 1, tq), F32), pltpu.VMEM((nh, VT_ROWS, tq), F32)],
            compiler_params=_cparams(("arbitrary", "arbitrary", "arbitrary")),
            name="mla_attention_online" if online else "mla_attention",
        )(q, k, vt)

    return lax.cond(logit_bound <= FIXED_REF_MAX_BOUND, lambda: run(False), lambda: run(True))


def _split3(a):
    hi = a.astype(BF16)
    r1 = a - hi.astype(F32)
    mid = r1.astype(BF16)
    lo = (r1 - mid.astype(F32)).astype(BF16)
    return hi, mid, lo


def _split2(a):
    hi = a.astype(BF16)
    lo = (a - hi.astype(F32)).astype(BF16)
    return hi, lo


def _dot(a, b):
    return jnp.dot(a, b, preferred_element_type=F32)


def _dot_exact_lhs(a_bf16, b):
    hi, mid, lo = _split3(b)
    return _dot(a_bf16, hi) + _dot(a_bf16, mid) + _dot(a_bf16, lo)


def _gdn_body(qkv_ref, beta_ref, g_ref, z_ref, onorm_ref, o_ref, state_ref, *, tc):
    c = GDN_CHUNK
    npairs = N_HEADS // 2

    @pl.when(pl.program_id(1) == 0)
    def _():
        state_ref[...] = jnp.zeros(state_ref.shape, F32)

    ri = lax.broadcasted_iota(jnp.int32, (c, LANES), 0)
    li = lax.broadcasted_iota(jnp.int32, (c, LANES), 1)
    lj = li & (c - 1)
    tril2 = lj <= ri
    strict2 = lj < ri
    eye2 = (lj == ri).astype(F32)
    r2 = lax.broadcasted_iota(jnp.int32, (2 * c, LANES), 0)
    l2 = lax.broadcasted_iota(jnp.int32, (2 * c, LANES), 1)
    blkmask = (r2 < c) == (l2 < c)
    gsum_m = blkmask.astype(BF16)
    rr = lax.broadcasted_iota(jnp.int32, (c, c), 0)
    cc = lax.broadcasted_iota(jnp.int32, (c, c), 1)
    ltri = (cc <= rr).astype(BF16)
    first_half = li < c

    def blk(rp):
        return jnp.where(blkmask, jnp.concatenate([rp, rp], axis=0), 0.0)

    def mm(lp, rp):
        return _dot(lp.astype(BF16), blk(rp).astype(BF16))

    mmi = mm

    def gsum(y):
        hi, lo = _split2(y)
        return _dot(hi, gsum_m) + _dot(lo, gsum_m)

    nt = (((1,), (1,)), ((), ()))
    tn = (((0,), (0,)), ((), ()))
    level_masks = []
    for sh in (3, 4, 5):
        same = (ri >> (sh + 1)) == (lj >> (sh + 1))
        level_masks.append(same & (((ri >> sh) & 1) == 1) & (((lj >> sh) & 1) == 0))
    diag8 = ((ri >> 3) == (lj >> 3)) & strict2

    nch = tc // c
    probs = [(ch, p) for ch in range(nch) for p in range(npairs)]
    rows = lambda ch: slice(ch * c, (ch + 1) * c)
    ls = lambda p: slice(p * LANES, (p + 1) * LANES)
    qn, kn, kb, vb, gcum, decay, egc = {}, {}, {}, {}, {}, {}, {}
    for pr in probs:
        ch, p = pr
        q = qkv_ref[rows(ch), p * LANES:(p + 1) * LANES]
        k = qkv_ref[rows(ch), GDN_W + p * LANES:GDN_W + (p + 1) * LANES]
        v = qkv_ref[rows(ch), 2 * GDN_W + p * LANES:2 * GDN_W + (p + 1) * LANES]
        bt = beta_ref[rows(ch), ls(p)]
        gg = g_ref[rows(ch), ls(p)]
        qn[pr] = q * lax.rsqrt(gsum(q * q) + EPS) * (GDN_DK ** -0.5)
        kn[pr] = k * lax.rsqrt(gsum(k * k) + EPS)
        cs = _dot_exact_lhs(ltri, jnp.concatenate([gg, jnp.where(strict2, gg, 0.0)], axis=1))
        gcum[pr] = cs[:, :LANES]
        decay[pr] = jnp.exp(jnp.where(tril2, cs[:, LANES:], -jnp.inf))
        egc[pr] = jnp.exp(gcum[pr])
        kb[pr] = kn[pr] * bt
        vb[pr] = v * bt
    n, attn = {}, {}
    for pr in probs:
        lhs = jnp.concatenate([kb[pr], qn[pr]], axis=0).astype(BF16)
        res = lax.dot_general(lhs, blk(kn[pr]).astype(BF16), nt, preferred_element_type=F32)
        n[pr] = jnp.where(strict2, res[:c] * decay[pr], 0.0)
        attn[pr] = res[c:] * decay[pr]
    nd = {pr: jnp.where(diag8, n[pr], 0.0) for pr in probs}
    n2 = {pr: mmi(nd[pr], nd[pr]) for pr in probs}
    n4 = {pr: mmi(n2[pr], n2[pr]) for pr in probs}
    tinv = {pr: mmi(eye2 - nd[pr], eye2 + n2[pr]) for pr in probs}
    tinv = {pr: mmi(tinv[pr], eye2 + n4[pr]) for pr in probs}
    for lm in level_masks:
        xk = {pr: mmi(tinv[pr], jnp.where(lm, n[pr], 0.0)) for pr in probs}
        tinv = {pr: tinv[pr] - mmi(xk[pr], tinv[pr]) for pr in probs}
    u, w = {}, {}
    for pr in probs:
        rhs = jnp.concatenate([blk(vb[pr]), blk(kb[pr] * egc[pr])], axis=1).astype(BF16)
        uw = _dot(tinv[pr].astype(BF16), rhs)
        u[pr] = uw[:, :LANES]
        w[pr] = uw[:, LANES:]
    st = [state_ref[p] for p in range(npairs)]
    for ch in range(nch):
        for p in range(npairs):
            pr = (ch, p)
            glast = gcum[pr][c - 1:c, :]
            ktail = kn[pr] * jnp.exp(glast - gcum[pr])
            wq = jnp.concatenate([w[pr], qn[pr] * egc[pr]], axis=0).astype(BF16)
            ws = _dot(wq, blk(st[p]).astype(BF16))
            vnew = u[pr] - ws[:c]
            o = ws[c:] + mm(attn[pr], vnew)
            upd = lax.dot_general(ktail.astype(BF16), vnew.astype(BF16), tn, preferred_element_type=F32)
            st[p] = st[p] * jnp.exp(glast) + jnp.where(first_half, upd[0:c, :], upd[c:2 * c, :])
            ms = gsum(o * o) * (1.0 / GDN_DV)
            zz = z_ref[rows(ch), ls(p)]
            y = o * lax.rsqrt(ms + EPS) * onorm_ref[...] * (zz * jax.nn.sigmoid(zz))
            o_ref[rows(ch), ls(p)] = y.astype(o_ref.dtype)
    for p in range(npairs):
        state_ref[p] = st[p]


def _gdn(qkv, beta, g, z, onorm2, *, batch, seq, tc=512):
    t = qkv.shape[0]
    nb = seq // tc
    row = lambda b, i: (b * nb + i, 0)
    return pl.pallas_call(
        functools.partial(_gdn_body, tc=tc),
        out_shape=jax.ShapeDtypeStruct((t, GDN_W), BF16),
        grid=(batch, nb),
        in_specs=[pl.BlockSpec((tc, 3 * GDN_W), row), pl.BlockSpec((tc, GDN_W), row),
                  pl.BlockSpec((tc, GDN_W), row), pl.BlockSpec((tc, GDN_W), row),
                  _const_spec((1, LANES))],
        out_specs=pl.BlockSpec((tc, GDN_W), row),
        scratch_shapes=[pltpu.VMEM((N_HEADS // 2, GDN_DK, LANES), F32)],
        compiler_params=_cparams(("arbitrary", "arbitrary")),
        name="gated_delta_rule",
    )(qkv, beta, g, z, onorm2)


def _sg_body(x_ref, nw_ref, win_ref, vn_ref, ws_ref, bs_ref, wout_ref, o_ref, gate_ref, *, tm):
    rr = lax.broadcasted_iota(jnp.int32, (SG_CHUNK, SG_CHUNK), 0)
    cc = lax.broadcasted_iota(jnp.int32, (SG_CHUNK, SG_CHUNK), 1)
    causal = cc <= rr
    wc = [jnp.where(causal, ws_ref[g], 0.0).astype(BF16) for g in range(SG_GROUPS)]
    for s in range(tm // SG_SUB):
        rows = slice(s * SG_SUB, (s + 1) * SG_SUB)
        x = x_ref[rows, :]
        hb = _rms(x, nw_ref[...]).astype(BF16)
        uv = _dot(hb, win_ref[...])
        uv = 0.5 * uv * (1.0 + lax.erf(uv * (2.0 ** -0.5)))
        for g in range(SG_GROUPS):
            cs = slice(SG_WIDTH + g * SG_CG, SG_WIDTH + (g + 1) * SG_CG)
            vg = uv[:, cs]
            vg = vg * lax.rsqrt(jnp.mean(vg * vg, axis=-1, keepdims=True) + EPS) * vn_ref[:, g * SG_CG:(g + 1) * SG_CG]
            vg = vg.astype(BF16)
            bias = bs_ref[g]
            for n in range(SG_SUB // SG_CHUNK):
                rs = slice(n * SG_CHUNK, (n + 1) * SG_CHUNK)
                gs = slice(s * SG_SUB + n * SG_CHUNK, s * SG_SUB + (n + 1) * SG_CHUNK)
                gate_ref[gs, g * SG_CG:(g + 1) * SG_CG] = _dot(wc[g], vg[rs, :]) + bias
        y = (uv[:, :SG_WIDTH] * gate_ref[rows, :]).astype(BF16)
        o_ref[rows, :] = x + _dot(y, wout_ref[...])


def _sg_mixer(x, nw, win, vn, ws, bs, wout, *, tm=512):
    t = x.shape[0]
    row = lambda i: (i, 0)
    consts = [nw, win, vn, ws, bs, wout]
    return pl.pallas_call(
        functools.partial(_sg_body, tm=tm),
        out_shape=jax.ShapeDtypeStruct((t, D_MODEL), F32),
        grid=(t // tm,),
        in_specs=[pl.BlockSpec((tm, D_MODEL), row)] + [_const_spec(c.shape) for c in consts],
        out_specs=pl.BlockSpec((tm, D_MODEL), row),
        scratch_shapes=[pltpu.VMEM((tm, SG_WIDTH), F32)],
        compiler_params=_cparams(("arbitrary",)),
        name="spatial_gating",
    )(x, *consts)


def _pad_heads(w, width, pad_to):
    k = w.shape[0]
    w = w.reshape(k, N_HEADS, width)
    return jnp.pad(w, ((0, 0), (0, 0), (0, pad_to - width))).reshape(k, N_HEADS * pad_to)


def _rot_partner(w_rope):
    half = MLA_ROPE // 2
    return jnp.concatenate([-w_rope[..., half:], w_rope[..., :half]], axis=-1)


def _place_rope(w_rope):
    return jnp.pad(w_rope, ((0, 0), (MLA_NOPE, LANES - MLA_QK)))


def _even_weights(l, i, norm_w, ab_w_in, mla_q_norm, mla_w_q_b, mla_kv_norm, mla_w_kv_b, mla_qk_norm_q,
                  mla_qk_norm_k, gdn_conv_w, gdn_a_log, gdn_dt_bias, gdn_out_norm, ab_w_out):
    w_in = ab_w_in[i]
    o_kr = MLA_Q_LORA + MLA_KV_LORA
    o_qkv = o_kr + MLA_ROPE
    o_z = o_qkv + 3 * GDN_W
    o_b = o_z + GDN_W
    o_a = o_b + N_HEADS
    w_kr = w_in[:, o_kr:o_qkv]
    w1 = jnp.concatenate([
        w_in[:, :o_kr],
        _place_rope(w_kr),
        _place_rope(_rot_partner(w_kr)),
        w_in[:, o_qkv:o_b],
        jnp.repeat(w_in[:, o_b:o_a], GDN_DK, axis=1),
        jnp.repeat(w_in[:, o_a:o_a + N_HEADS], GDN_DK, axis=1),
    ], axis=1).astype(BF16)
    wq = mla_w_q_b[i].reshape(MLA_Q_LORA, N_HEADS, MLA_QK)
    wq_part = jnp.concatenate([jnp.zeros_like(wq[..., :MLA_NOPE]), _rot_partner(wq[..., MLA_NOPE:])], axis=-1)
    wq2 = jnp.concatenate([
        _pad_heads(wq.reshape(MLA_Q_LORA, -1), MLA_QK, HEAD_PAD),
        _pad_heads(wq_part.reshape(MLA_Q_LORA, -1), MLA_QK, HEAD_PAD),
    ], axis=1).astype(BF16)
    wkv = mla_w_kv_b[i].reshape(MLA_KV_LORA, N_HEADS, MLA_NOPE + MLA_V)
    wkv2 = _pad_heads(wkv[..., :MLA_NOPE].reshape(MLA_KV_LORA, -1), MLA_NOPE, HEAD_PAD).astype(BF16)
    wvt = _pad_heads(wkv[..., MLA_NOPE:].reshape(MLA_KV_LORA, -1), MLA_V, VT_ROWS).T.astype(BF16)

    def norm_pair(nvec):
        full = jnp.pad(nvec, (0, LANES - MLA_QK)).reshape(1, LANES)
        rope = nvec[MLA_NOPE:]
        half = MLA_ROPE // 2
        part = jnp.concatenate([rope[half:], rope[:half]])
        part = jnp.pad(part, (MLA_NOPE, LANES - MLA_QK)).reshape(1, LANES)
        return full, part

    bound = MLA_QK * _Q_FOLD * jnp.max(jnp.abs(mla_qk_norm_q[i])) * jnp.max(jnp.abs(mla_qk_norm_k[i]))
    bound = bound.astype(BF16).astype(F32)
    lane = jnp.arange(LANES) == SHIFT_LANE
    qshift = jnp.where(lane, -bound, 0.0).reshape(1, LANES).astype(F32)
    kone = lane.astype(F32).reshape(1, LANES)
    nq, nqp = norm_pair(mla_qk_norm_q[i])
    nk, nkp = norm_pair(mla_qk_norm_k[i])
    w_out = ab_w_out[i].astype(BF16)
    return dict(
        nw=norm_w[l, 1].reshape(1, D_MODEL), w1=w1, q_norm=mla_q_norm[i].reshape(1, -1), wq=wq2,
        kv_norm=mla_kv_norm[i].reshape(1, -1), wkv=wkv2, wvt=wvt, nq=nq, nqp=nqp, nk=nk, nkp=nkp,
        qshift=qshift, kone=kone, logit_bound=bound,
        conv_w=gdn_conv_w[i], a_log=jnp.repeat(gdn_a_log[i], GDN_DK).reshape(1, GDN_W),
        dt_bias=jnp.repeat(gdn_dt_bias[i], GDN_DK).reshape(1, GDN_W),
        onorm2=jnp.tile(gdn_out_norm[i], 2).reshape(1, LANES),
        w_out_a=w_out[:N_HEADS * MLA_V], w_out_b=w_out[N_HEADS * MLA_V:],
    )


def kernel(x, positions, norm_w, ffn_w_gate, ffn_w_up, ffn_w_down, ab_w_in, mla_q_norm, mla_w_q_b, mla_kv_norm, mla_w_kv_b, mla_qk_norm_q, mla_qk_norm_k, gdn_conv_w, gdn_a_log, gdn_dt_bias, gdn_out_norm, ab_w_out, sg_w_in, sg_v_norm, sg_w_s, sg_b_s, sg_w_out):
    batch, seq, _ = x.shape
    depth = norm_w.shape[0]
    t = batch * seq
    xf = x.reshape(t, D_MODEL)
    cos_full, sin_full = _rope_tables(positions)

    wg = ffn_w_gate.astype(BF16)
    wu = ffn_w_up.astype(BF16)
    wd = ffn_w_down.astype(BF16)

    def ffn(xf, l, j, slot, proj=None):
        return _ffn(xf, norm_w[l, slot].reshape(1, D_MODEL), wg, wu, wd, l, j, proj=proj)

    for l in range(depth):
        i = l // 2
        xf = ffn(xf, l, 0, 0)
        proj = None
        if l % 2 == 0:
            wts = _even_weights(l, i, norm_w, ab_w_in, mla_q_norm, mla_w_q_b, mla_kv_norm, mla_w_kv_b,
                                mla_qk_norm_q, mla_qk_norm_k, gdn_conv_w, gdn_a_log, gdn_dt_bias,
                                gdn_out_norm, ab_w_out)
            q, k, vt, qkv, beta, g, z = _ab_in(xf, cos_full, sin_full, wts, seq=seq)
            y_mla = _attention(q, k, vt, wts["logit_bound"], batch=batch, seq=seq)
            y_gdn = _gdn(qkv, beta, g, z, wts["onorm2"], batch=batch, seq=seq)
            proj = (y_mla, y_gdn, wts["w_out_a"], wts["w_out_b"])
        else:
            bs = jnp.broadcast_to(sg_b_s[i][:, :, None], (SG_GROUPS, SG_CHUNK, SG_CG))
            xf = _sg_mixer(xf, norm_w[l, 1].reshape(1, D_MODEL), sg_w_in[i].astype(BF16),
                           sg_v_norm[i].reshape(1, SG_WIDTH), sg_w_s[i], bs, sg_w_out[i].astype(BF16))
        xf = ffn(xf, l, 1, 2, proj=proj)
    return xf.reshape(batch, seq, D_MODEL)
```

```python
import functools

import numpy as np
import jax
import jax.numpy as jnp
from jax import lax
from jax.experimental import pallas as pl
from jax.experimental.pallas import tpu as pltpu

F32 = jnp.float32
BF16 = jnp.bfloat16

D_MODEL = 1024
D_FF = 2816
EPS = 1e-6
N_HEADS = 8
MLA_Q_LORA = 256
MLA_KV_LORA = 128
MLA_NOPE = 64
MLA_ROPE = 32
MLA_V = 64
MLA_QK = MLA_NOPE + MLA_ROPE
ROPE_BASE = 10000.0
GDN_DK = 64
GDN_DV = 64
GDN_CONV = 4
GDN_CHUNK = 64
GDN_W = N_HEADS * GDN_DK
SG_GROUPS = 8
SG_WIDTH = 2048
SG_CHUNK = 128
SG_CG = SG_WIDTH // SG_GROUPS

LANES = 128
MXU_TILE = 256
VMEM_LIMIT_BYTES = 56 * 1024 * 1024
HEAD_PAD = LANES
VT_ROWS = 128
SHIFT_LANE = MLA_QK
FIXED_REF_MAX_BOUND = 60.0
GDN_GROUP = 4
SG_SUB = 256
AB_SUB = 256
_C_Q0 = 0
_C_KV0 = _C_Q0 + MLA_Q_LORA
_C_KR0 = _C_KV0 + MLA_KV_LORA
_C_KRP0 = _C_KR0 + LANES
_C_QKV0 = _C_KRP0 + LANES
_C_Z0 = _C_QKV0 + 3 * GDN_W
_C_B0 = _C_Z0 + GDN_W
_C_A0 = _C_B0 + GDN_W
IN_AB_PAD = _C_A0 + GDN_W

_Q_FOLD = (MLA_QK ** -0.5) * float(np.log2(np.e))


def _cparams(sem):
    return pltpu.CompilerParams(dimension_semantics=sem, vmem_limit_bytes=VMEM_LIMIT_BYTES)


def _rms(x, w):
    return x * lax.rsqrt(jnp.mean(x * x, axis=-1, keepdims=True) + EPS) * w


def _const_spec(shape):
    nd = len(shape)
    return pl.BlockSpec(shape, lambda *_: (0,) * nd, pipeline_mode=pl.Buffered(1))


def _ffn_chunks():
    tiles = D_FF // MXU_TILE
    first = (tiles + 1) // 2 * MXU_TILE
    return ((0, first), (first, D_FF))


def _swiglu_residual(x, nw_ref, wg_ref, wu_ref, wd_ref):
    h = _rms(x, nw_ref[...]).astype(BF16)
    acc = None
    for lo, hi in _ffn_chunks():
        sl = slice(lo, hi)
        g = jnp.dot(h, wg_ref[:, sl], preferred_element_type=F32)
        u = jnp.dot(h, wu_ref[:, sl], preferred_element_type=F32)
        a = (g * jax.nn.sigmoid(g) * u).astype(BF16)
        d = jnp.dot(a, wd_ref[sl, :], preferred_element_type=F32)
        acc = d if acc is None else acc + d
    return x + 0.5 * acc


def _ffn_body(x_ref, nw_ref, wg_ref, wu_ref, wd_ref, o_ref):
    o_ref[...] = _swiglu_residual(x_ref[...], nw_ref, wg_ref, wu_ref, wd_ref)


def _proj_ffn_body(x_ref, ya_ref, yb_ref, wa_ref, wb_ref, nw_ref, wg_ref, wu_ref, wd_ref, o_ref):
    x = x_ref[...] + _dot(ya_ref[...], wa_ref[...]) + _dot(yb_ref[...], wb_ref[...])
    o_ref[...] = _swiglu_residual(x, nw_ref, wg_ref, wu_ref, wd_ref)


def _ffn(x, nw, wg, wu, wd, l, j, *, proj=None, tm=512):
    t = x.shape[0]
    row = lambda i: (i, 0)

    def stacked_spec(w):
        return pl.BlockSpec((None, None) + w.shape[2:], lambda i: (l, j, 0, 0), pipeline_mode=pl.Buffered(1))

    ffn_consts = [nw, wg, wu, wd]
    ffn_specs = [_const_spec(nw.shape), stacked_spec(wg), stacked_spec(wu), stacked_spec(wd)]
    if proj is None:
        body, acts, consts, cspecs = _ffn_body, [x], ffn_consts, ffn_specs
    else:
        ya, yb, wa, wb = proj
        body, acts, consts = _proj_ffn_body, [x, ya, yb], [wa, wb] + ffn_consts
        cspecs = [_const_spec(wa.shape), _const_spec(wb.shape)] + ffn_specs
    return pl.pallas_call(
        body,
        out_shape=jax.ShapeDtypeStruct((t, D_MODEL), F32),
        grid=(t // tm,),
        in_specs=[pl.BlockSpec((tm, a.shape[1]), row) for a in acts] + cspecs,
        out_specs=pl.BlockSpec((tm, D_MODEL), row),
        compiler_params=_cparams(("arbitrary",)),
        name="ffn" if proj is None else "proj_ffn",
    )(*acts, *consts)


def _rope_body(pos_ref, invf_ref, cos_ref, sin_ref):
    ang = pos_ref[...] * invf_ref[...]
    cos_ref[...] = jnp.cos(ang)
    sin_ref[...] = jnp.sin(ang)


def _rope_tables(positions):
    b, s = positions.shape
    t = b * s
    half = MLA_ROPE // 2
    inv_freq = ROPE_BASE ** (-jnp.arange(0, MLA_ROPE, 2, dtype=F32) / MLA_ROPE)
    per_row = LANES // half
    pos = jnp.repeat(positions.reshape(t).astype(F32), half).reshape(t // per_row, LANES)
    invf = jnp.tile(inv_freq, per_row).reshape(1, LANES)
    rows = t // per_row
    tr = min(rows, 512)
    cos, sin = pl.pallas_call(
        _rope_body,
        out_shape=(jax.ShapeDtypeStruct((rows, LANES), F32),) * 2,
        grid=(rows // tr,),
        in_specs=[pl.BlockSpec((tr, LANES), lambda i: (i, 0)), _const_spec((1, LANES))],
        out_specs=(pl.BlockSpec((tr, LANES), lambda i: (i, 0)),) * 2,
        compiler_params=_cparams(("arbitrary",)),
        name="rope_tables",
    )(pos, invf)
    cos = cos.reshape(t, half)
    sin = sin.reshape(t, half)
    pad = LANES - MLA_QK
    cos_full = jnp.concatenate([jnp.ones((t, MLA_NOPE), F32), cos, cos, jnp.ones((t, pad), F32)], axis=-1)
    sin_full = jnp.concatenate([jnp.zeros((t, MLA_NOPE), F32), sin, sin, jnp.zeros((t, pad), F32)], axis=-1)
    return cos_full, sin_full


def _softplus(x):
    return jnp.maximum(x, 0.0) + jnp.log1p(jnp.exp(-jnp.abs(x)))


def _ab_in_body(x_ref, cos_ref, sin_ref, nw_ref, w1_ref, qn_ref, wq_ref, kvn_ref, wkv_ref, wvt_ref,
                nq_ref, nqp_ref, nk_ref, nkp_ref, qshift_ref, kone_ref, convw_ref, alog_ref, dtb_ref,
                q_out, k_out, vt_out, qkv_out, beta_out, g_out, z_out,
                carry_ref, *, tm, tiles_per_seq):
    i = pl.program_id(0)

    @pl.when(i % tiles_per_seq == 0)
    def _():
        carry_ref[0:8, :] = jnp.zeros((8, 3 * GDN_W), F32)

    hw = N_HEADS * HEAD_PAD
    nsub = tm // AB_SUB
    rows = [slice(s * AB_SUB, (s + 1) * AB_SUB) for s in range(nsub)]
    p = [jnp.dot(_rms(x_ref[r, :], nw_ref[...]).astype(BF16), w1_ref[...], preferred_element_type=F32)
         for r in rows]

    qq, kk = [], []
    for s in range(nsub):
        cqn = _rms(p[s][:, _C_Q0:_C_KV0], qn_ref[...]).astype(BF16)
        qq.append(jnp.dot(cqn, wq_ref[...], preferred_element_type=F32))
        ckvn = _rms(p[s][:, _C_KV0:_C_KR0], kvn_ref[...]).astype(BF16)
        kk.append(jnp.dot(ckvn, wkv_ref[...], preferred_element_type=F32))
        vt = lax.dot_general(wvt_ref[...], ckvn, (((1,), (1,)), ((), ())), preferred_element_type=F32)
        vrow = lax.broadcasted_iota(jnp.int32, vt.shape, 0) & (VT_ROWS - 1)
        vt_out[0, :, rows[s]] = jnp.where(vrow == MLA_V, 1.0, vt).astype(BF16)
    for s in range(nsub):
        kr = p[s][:, _C_KR0:_C_KRP0]
        krp = p[s][:, _C_KRP0:_C_QKV0]
        cos = cos_ref[rows[s], :]
        sin = sin_ref[rows[s], :]
        for h in range(N_HEADS):
            sl = slice(h * HEAD_PAD, (h + 1) * HEAD_PAD)
            qh = qq[s][:, sl]
            qph = qq[s][:, hw + h * HEAD_PAD: hw + (h + 1) * HEAD_PAD]
            rq = lax.rsqrt(jnp.sum(qh * qh, axis=-1, keepdims=True) * (1.0 / MLA_QK) + EPS)
            qr = (qh * nq_ref[...] * cos + qph * nqp_ref[...] * sin) * (rq * _Q_FOLD)
            q_out[rows[s], sl] = (qr + qshift_ref[...]).astype(BF16)
            kh = kk[s][:, sl] + kr
            rk = lax.rsqrt(jnp.sum(kh * kh, axis=-1, keepdims=True) * (1.0 / MLA_QK) + EPS)
            kro = (kh * nk_ref[...] * cos + krp * nkp_ref[...] * sin) * rk
            k_out[rows[s], sl] = (kro + kone_ref[...]).astype(BF16)

    for s in range(nsub):
        qkv_pre = p[s][:, _C_QKV0:_C_Z0]
        base = 8 + s * AB_SUB
        carry_ref[base:base + AB_SUB, :] = qkv_pre
        conv = qkv_pre * convw_ref[GDN_CONV - 1:GDN_CONV, :]
        for j in range(GDN_CONV - 1):
            off = base - (GDN_CONV - 1) + j
            conv = conv + carry_ref[off:off + AB_SUB, :] * convw_ref[j:j + 1, :]
        qkv_out[rows[s], :] = conv * jax.nn.sigmoid(conv)
        beta_out[rows[s], :] = jax.nn.sigmoid(p[s][:, _C_B0:_C_A0])
        g_out[rows[s], :] = -jnp.exp(alog_ref[...]) * _softplus(p[s][:, _C_A0:IN_AB_PAD] + dtb_ref[...])
        z_out[rows[s], :] = p[s][:, _C_Z0:_C_B0]
    carry_ref[0:8, :] = carry_ref[tm:tm + 8, :]


def _ab_in(x, cos_full, sin_full, wts, *, seq, tm=512):
    t = x.shape[0]
    tiles_per_seq = seq // tm
    row = lambda i: (i, 0)
    hw = N_HEADS * HEAD_PAD
    outs = (
        jax.ShapeDtypeStruct((t, hw), BF16),
        jax.ShapeDtypeStruct((t, hw), BF16),
        jax.ShapeDtypeStruct((t // tm, N_HEADS * VT_ROWS, tm), BF16),
        jax.ShapeDtypeStruct((t, 3 * GDN_W), F32),
        jax.ShapeDtypeStruct((t, GDN_W), F32),
        jax.ShapeDtypeStruct((t, GDN_W), F32),
        jax.ShapeDtypeStruct((t, GDN_W), F32),
    )
    consts = [wts["nw"], wts["w1"], wts["q_norm"], wts["wq"], wts["kv_norm"], wts["wkv"], wts["wvt"],
              wts["nq"], wts["nqp"], wts["nk"], wts["nkp"], wts["qshift"], wts["kone"], wts["conv_w"], wts["a_log"], wts["dt_bias"]]
    out_specs = tuple(
        pl.BlockSpec((1,) + o.shape[1:], lambda i: (i, 0, 0)) if len(o.shape) == 3
        else pl.BlockSpec((tm, o.shape[1]), row) for o in outs)
    return pl.pallas_call(
        functools.partial(_ab_in_body, tm=tm, tiles_per_seq=tiles_per_seq),
        out_shape=outs,
        grid=(t // tm,),
        in_specs=[pl.BlockSpec((tm, D_MODEL), row), pl.BlockSpec((tm, LANES), row),
                  pl.BlockSpec((tm, LANES), row)] + [_const_spec(c.shape) for c in consts],
        out_specs=out_specs,
        scratch_shapes=[pltpu.VMEM((tm + 8, 3 * GDN_W), F32)],
        compiler_params=_cparams(("arbitrary",)),
        name="ab_in",
    )(x, cos_full, sin_full, *consts)


def _attn_body(q_ref, k_ref, vt_ref, o_ref, m_ref, acc_ref, *, tq, tk, vtile, nh, online):
    qi = pl.program_id(2)
    nt = (((1,), (1,)), ((), ()))
    sub = tk // vtile
    if online:
        m_ref[...] = jnp.full(m_ref.shape, -jnp.inf, F32)
    acc_ref[...] = jnp.zeros(acc_ref.shape, F32)

    def block(kb, masked):
        row0 = pl.multiple_of(kb * tk, tk)
        def scores(e):
            q = q_ref[:, e * HEAD_PAD:(e + 1) * HEAD_PAD]
            k = k_ref[pl.ds(row0, tk), e * HEAD_PAD:(e + 1) * HEAD_PAD]
            s = lax.dot_general(k, q, nt, preferred_element_type=F32)
            if masked:
                key = kb * tk + lax.broadcasted_iota(jnp.int32, (tk, tq), 0)
                qry = qi * tq + lax.broadcasted_iota(jnp.int32, (tk, tq), 1)
                s = jnp.where(key <= qry, s, -jnp.inf)
            return s

        st = [scores(e) for e in range(nh)]
        if online:
            m_new = [jnp.maximum(m_ref[e], jnp.max(st[e], axis=0, keepdims=True)) for e in range(nh)]
            alpha = [jnp.exp2(m_ref[e] - m_new[e]) for e in range(nh)]
            pt = [jnp.exp2(st[e] - m_new[e]).astype(BF16) for e in range(nh)]
        else:
            pt = [jnp.exp2(st[e]).astype(BF16) for e in range(nh)]
        for e in range(nh):
            pv = None
            for j in range(sub):
                vt = vt_ref[kb * sub + j, e * VT_ROWS:(e + 1) * VT_ROWS, :]
                d = _dot(vt, pt[e][j * vtile:(j + 1) * vtile, :])
                pv = d if pv is None else pv + d
            if online:
                m_ref[e] = m_new[e]
                acc_ref[e] = alpha[e] * acc_ref[e] + pv
            else:
                acc_ref[e] = acc_ref[e] + pv

    def body(kb, carry):
        block(kb, False)
        return carry

    nfull = (qi * tq) // tk
    lax.fori_loop(0, nfull, body, 0)
    for j in range(max(tq // tk, 1)):
        block(nfull + j, True)
    for pr in range(nh // 2):
        halves = []
        for e in (2 * pr, 2 * pr + 1):
            acc = acc_ref[e]
            halves.append(acc[:MLA_V] * (1.0 / acc[MLA_V:MLA_V + 1]))
        o_ref[:, pr * 2 * MLA_V:(pr + 1) * 2 * MLA_V] = jnp.concatenate(halves, axis=0).T.astype(o_ref.dtype)


def _attention(q, k, vt, logit_bound, *, batch, seq, tq=512, tk=512, nh=4):
    t = q.shape[0]
    vtile = vt.shape[2]
    assert tk % vtile == 0 and (tq % tk == 0 or tk % tq == 0) and seq % tk == 0 and seq % tq == 0
    nq = seq // tq
    nv = seq // vtile

    def run(online):
        return pl.pallas_call(
            functools.partial(_attn_body, tq=tq, tk=tk, vtile=vtile, nh=nh, online=online),
            out_shape=jax.ShapeDtypeStruct((t, N_HEADS * MLA_V), BF16),
            grid=(batch, N_HEADS // nh, nq),
            in_specs=[
                pl.BlockSpec((tq, nh * HEAD_PAD), lambda b, p, i: (b * nq + i, p)),
                pl.BlockSpec((seq, nh * HEAD_PAD), lambda b, p, i: (b, p)),
                pl.BlockSpec((nv, nh * VT_ROWS, vtile), lambda b, p, i: (b, p, 0)),
            ],
            out_specs=pl.BlockSpec((tq, nh * MLA_V), lambda b, p, i: (b * nq + i, p)),
            scratch_shapes=[pltpu.VMEM((nh, 1, tq), F32), pltpu.VMEM((nh, VT_ROWS, tq), F32)],
            compiler_params=_cparams(("arbitrary", "arbitrary", "arbitrary")),
            name="mla_attention_online" if online else "mla_attention",
        )(q, k, vt)

    return lax.cond(logit_bound <= FIXED_REF_MAX_BOUND, lambda: run(False), lambda: run(True))


def _dot(a, b):
    return jnp.dot(a, b, preferred_element_type=F32)


def _dot_exact_lhs(a_bf16, b):
    hi = b.astype(BF16)
    lo = (b - hi.astype(F32)).astype(BF16)
    return _dot(a_bf16, hi) + _dot(a_bf16, lo)


def _gdn_body(qkv_ref, beta_ref, g_ref, z_ref, onorm_ref, o_ref, state_ref, *, tc):
    c = GDN_CHUNK
    npairs = N_HEADS // 2

    @pl.when(pl.program_id(1) == 0)
    def _():
        state_ref[...] = jnp.zeros(state_ref.shape, F32)

    ri = lax.broadcasted_iota(jnp.int32, (c, LANES), 0)
    li = lax.broadcasted_iota(jnp.int32, (c, LANES), 1)
    lj = li & (c - 1)
    tril2 = lj <= ri
    strict2 = lj < ri
    eye2 = (lj == ri).astype(F32)
    r2 = lax.broadcasted_iota(jnp.int32, (2 * c, LANES), 0)
    l2 = lax.broadcasted_iota(jnp.int32, (2 * c, LANES), 1)
    blkmask = (r2 < c) == (l2 < c)
    gsum_m = blkmask.astype(BF16)
    r4 = lax.broadcasted_iota(jnp.int32, (2 * LANES, 2 * LANES), 0)
    l4 = lax.broadcasted_iota(jnp.int32, (2 * LANES, 2 * LANES), 1)
    gsum2_m = ((r4 >> 6) == (l4 >> 6)).astype(BF16)
    rr = lax.broadcasted_iota(jnp.int32, (c, c), 0)
    cc = lax.broadcasted_iota(jnp.int32, (c, c), 1)
    ltri = (cc <= rr).astype(BF16)
    first_half = li < c

    def blk(rp):
        return jnp.where(blkmask, jnp.concatenate([rp, rp], axis=0), 0.0)

    def mm(lp, rp):
        return _dot(lp.astype(BF16), blk(rp).astype(BF16))


    nt = (((1,), (1,)), ((), ()))
    tn = (((0,), (0,)), ((), ()))
    level_masks = []
    for sh in (3, 4, 5):
        same = (ri >> (sh + 1)) == (lj >> (sh + 1))
        level_masks.append(same & (((ri >> sh) & 1) == 1) & (((lj >> sh) & 1) == 0))
    diag8 = ((ri >> 3) == (lj >> 3)) & strict2

    nch = tc // c
    rows = lambda ch: slice(ch * c, (ch + 1) * c)
    ls = lambda p: slice(p * LANES, (p + 1) * LANES)
    qn_t, kn_t = [], []
    for p in range(npairs):
        q = qkv_ref[:, p * LANES:(p + 1) * LANES]
        k = qkv_ref[:, GDN_W + p * LANES:GDN_W + (p + 1) * LANES]
        ssq = _dot(jnp.concatenate([q * q, k * k], axis=1).astype(BF16), gsum2_m)
        qn_t.append(q * lax.rsqrt(ssq[:, :LANES] + EPS) * (GDN_DK ** -0.5))
        kn_t.append(k * lax.rsqrt(ssq[:, LANES:] + EPS))

    v = {}
    st = [state_ref[p] for p in range(npairs)]
    outs = [[] for _ in range(npairs)]

    def prep_stages(chunks):
        probs = [(ch, p) for ch in chunks for p in range(npairs)]

        def load():
            for ch, p in probs:
                pr = (ch, p)
                bt = beta_ref[rows(ch), ls(p)]
                gg = g_ref[rows(ch), ls(p)]
                v["qn", pr] = qn_t[p][rows(ch), :]
                v["kn", pr] = kn_t[p][rows(ch), :]
                cs = _dot_exact_lhs(ltri, jnp.concatenate([gg, jnp.where(strict2, gg, 0.0)], axis=1))
                v["gcum", pr] = cs[:, :LANES]
                v["decay", pr] = jnp.exp(jnp.where(tril2, cs[:, LANES:], -jnp.inf))
                v["egc", pr] = jnp.exp(v["gcum", pr])
                v["kb", pr] = v["kn", pr] * bt
                v["vb", pr] = qkv_ref[rows(ch), 2 * GDN_W + p * LANES:2 * GDN_W + (p + 1) * LANES] * bt

        def gram():
            for pr in probs:
                lhs = jnp.concatenate([v["kb", pr], v["qn", pr]], axis=0).astype(BF16)
                res = lax.dot_general(lhs, blk(v["kn", pr]).astype(BF16), nt, preferred_element_type=F32)
                v["n", pr] = jnp.where(strict2, res[:c] * v["decay", pr], 0.0)
                v["attn", pr] = res[c:] * v["decay", pr]
                v["nd", pr] = jnp.where(diag8, v["n", pr], 0.0)

        def inv_a():
            for pr in probs:
                v["n2", pr] = mm(v["nd", pr], v["nd", pr])

        def inv_b():
            for pr in probs:
                lhs = jnp.concatenate([eye2 - v["nd", pr], v["n2", pr]], axis=0)
                v["t1n4", pr] = mm(lhs, eye2 + v["n2", pr])

        def inv_c():
            for pr in probs:
                v["tinv", pr] = mm(v["t1n4", pr][:c], eye2 + (v["t1n4", pr][c:] - v["n2", pr]))

        def merge_x(lm):
            def run():
                for pr in probs:
                    v["xk", pr] = mm(v["tinv", pr], jnp.where(lm, v["n", pr], 0.0))
            return run

        def merge_t():
            for pr in probs:
                v["tinv", pr] = v["tinv", pr] - mm(v["xk", pr], v["tinv", pr])

        def solve():
            for pr in probs:
                rhs = jnp.concatenate([blk(v["vb", pr]), blk(v["kb", pr] * v["egc", pr])], axis=1)
                uw = _dot(v["tinv", pr].astype(BF16), rhs.astype(BF16))
                v["u", pr] = uw[:, :LANES]
                glast = v["gcum", pr][c - 1:c, :]
                v["eglast", pr] = jnp.exp(glast)
                v["ktail", pr] = (v["kn", pr] * jnp.exp(glast - v["gcum", pr])).astype(BF16)
                v["wq", pr] = jnp.concatenate([uw[:, LANES:], v["qn", pr] * v["egc", pr]], axis=0).astype(BF16)

        stages = [load, gram, inv_a, inv_b, inv_c]
        for lm in level_masks:
            stages += [merge_x(lm), merge_t]
        return stages + [solve]

    def rec_stages(chunks):
        stages = []
        for ch in chunks:
            def step_a(ch=ch):
                for p in range(npairs):
                    v["ws", p] = _dot(v["wq", (ch, p)], blk(st[p]).astype(BF16))
                    v["vnew", p] = v["u", (ch, p)] - v["ws", p][:c]

            def step_b(ch=ch):
                for p in range(npairs):
                    upd = lax.dot_general(v["ktail", (ch, p)], v["vnew", p].astype(BF16), tn,
                                          preferred_element_type=F32)
                    st[p] = st[p] * v["eglast", (ch, p)] + jnp.where(first_half, upd[0:c, :], upd[c:2 * c, :])
                    outs[p].append(v["ws", p][c:] + mm(v["attn", (ch, p)], v["vnew", p]))

            stages += [step_a, step_b]
        return stages

    groups = [list(range(g0, g0 + GDN_GROUP)) for g0 in range(0, nch, GDN_GROUP)]
    for stage in prep_stages(groups[0]):
        stage()
    for gi in range(1, len(groups)):
        prep, rec = prep_stages(groups[gi]), rec_stages(groups[gi - 1])
        for k in range(max(len(prep), len(rec))):
            if k < len(rec):
                rec[k]()
            if k < len(prep):
                prep[k]()
    for stage in rec_stages(groups[-1]):
        stage()
    for p in range(npairs):
        state_ref[p] = st[p]
        o = jnp.concatenate(outs[p], axis=0)
        ms = _dot((o * o).astype(BF16), gsum_m) * (1.0 / GDN_DV)
        zz = z_ref[:, ls(p)]
        y = o * lax.rsqrt(ms + EPS) * onorm_ref[...] * (zz * jax.nn.sigmoid(zz))
        o_ref[:, ls(p)] = y.astype(o_ref.dtype)


def _gdn(qkv, beta, g, z, onorm2, *, batch, seq, tc=1024):
    t = qkv.shape[0]
    nb = seq // tc
    row = lambda b, i: (b * nb + i, 0)
    return pl.pallas_call(
        functools.partial(_gdn_body, tc=tc),
        out_shape=jax.ShapeDtypeStruct((t, GDN_W), BF16),
        grid=(batch, nb),
        in_specs=[pl.BlockSpec((tc, 3 * GDN_W), row), pl.BlockSpec((tc, GDN_W), row),
                  pl.BlockSpec((tc, GDN_W), row), pl.BlockSpec((tc, GDN_W), row),
                  _const_spec((1, LANES))],
        out_specs=pl.BlockSpec((tc, GDN_W), row),
        scratch_shapes=[pltpu.VMEM((N_HEADS // 2, GDN_DK, LANES), F32)],
        compiler_params=_cparams(("arbitrary", "arbitrary")),
        name="gated_delta_rule",
    )(qkv, beta, g, z, onorm2)


def _sg_body(x_ref, nw_ref, win_ref, vn_ref, ws_ref, bs_ref, wout_ref, o_ref, gate_ref, *, tm):
    rr = lax.broadcasted_iota(jnp.int32, (SG_CHUNK, SG_CHUNK), 0)
    cc = lax.broadcasted_iota(jnp.int32, (SG_CHUNK, SG_CHUNK), 1)
    causal = cc <= rr
    wc = [jnp.where(causal, ws_ref[g], 0.0).astype(BF16) for g in range(SG_GROUPS)]
    nsub = tm // SG_SUB
    rows = [slice(s * SG_SUB, (s + 1) * SG_SUB) for s in range(nsub)]
    uv = []
    for s in range(nsub):
        t = _dot(_rms(x_ref[rows[s], :], nw_ref[...]).astype(BF16), win_ref[...])
        uv.append(0.5 * t * (1.0 + lax.erf(t * (2.0 ** -0.5))))
    for s in range(nsub):
        for g in range(SG_GROUPS):
            cs = slice(SG_WIDTH + g * SG_CG, SG_WIDTH + (g + 1) * SG_CG)
            vg = uv[s][:, cs]
            vg = vg * lax.rsqrt(jnp.mean(vg * vg, axis=-1, keepdims=True) + EPS) * vn_ref[:, g * SG_CG:(g + 1) * SG_CG]
            vg = vg.astype(BF16)
            bias = bs_ref[g]
            for n in range(SG_SUB // SG_CHUNK):
                rs = slice(n * SG_CHUNK, (n + 1) * SG_CHUNK)
                gs = slice(s * SG_SUB + n * SG_CHUNK, s * SG_SUB + (n + 1) * SG_CHUNK)
                gate_ref[gs, g * SG_CG:(g + 1) * SG_CG] = _dot(wc[g], vg[rs, :]) + bias
    for s in range(nsub):
        y = (uv[s][:, :SG_WIDTH] * gate_ref[rows[s], :]).astype(BF16)
        o_ref[rows[s], :] = x_ref[rows[s], :] + _dot(y, wout_ref[...])


def _sg_mixer(x, nw, win, vn, ws, bs, wout, *, tm=512):
    t = x.shape[0]
    row = lambda i: (i, 0)
    consts = [nw, win, vn, ws, bs, wout]
    return pl.pallas_call(
        functools.partial(_sg_body, tm=tm),
        out_shape=jax.ShapeDtypeStruct((t, D_MODEL), F32),
        grid=(t // tm,),
        in_specs=[pl.BlockSpec((tm, D_MODEL), row)] + [_const_spec(c.shape) for c in consts],
        out_specs=pl.BlockSpec((tm, D_MODEL), row),
        scratch_shapes=[pltpu.VMEM((tm, SG_WIDTH), F32)],
        compiler_params=_cparams(("arbitrary",)),
        name="spatial_gating",
    )(x, *consts)


def _pad_heads(w, width, pad_to):
    k = w.shape[0]
    w = w.reshape(k, N_HEADS, width)
    return jnp.pad(w, ((0, 0), (0, 0), (0, pad_to - width))).reshape(k, N_HEADS * pad_to)


def _rot_partner(w_rope):
    half = MLA_ROPE // 2
    return jnp.concatenate([-w_rope[..., half:], w_rope[..., :half]], axis=-1)


def _place_rope(w_rope):
    return jnp.pad(w_rope, ((0, 0), (MLA_NOPE, LANES - MLA_QK)))


def _even_weights(l, i, norm_w, ab_w_in, mla_q_norm, mla_w_q_b, mla_kv_norm, mla_w_kv_b, mla_qk_norm_q,
                  mla_qk_norm_k, gdn_conv_w, gdn_a_log, gdn_dt_bias, gdn_out_norm, ab_w_out):
    w_in = ab_w_in[i]
    o_kr = MLA_Q_LORA + MLA_KV_LORA
    o_qkv = o_kr + MLA_ROPE
    o_z = o_qkv + 3 * GDN_W
    o_b = o_z + GDN_W
    o_a = o_b + N_HEADS
    w_kr = w_in[:, o_kr:o_qkv]
    w1 = jnp.concatenate([
        w_in[:, :o_kr],
        _place_rope(w_kr),
        _place_rope(_rot_partner(w_kr)),
        w_in[:, o_qkv:o_b],
        jnp.repeat(w_in[:, o_b:o_a], GDN_DK, axis=1),
        jnp.repeat(w_in[:, o_a:o_a + N_HEADS], GDN_DK, axis=1),
    ], axis=1).astype(BF16)
    wq = mla_w_q_b[i].reshape(MLA_Q_LORA, N_HEADS, MLA_QK)
    wq_part = jnp.concatenate([jnp.zeros_like(wq[..., :MLA_NOPE]), _rot_partner(wq[..., MLA_NOPE:])], axis=-1)
    wq2 = jnp.concatenate([
        _pad_heads(wq.reshape(MLA_Q_LORA, -1), MLA_QK, HEAD_PAD),
        _pad_heads(wq_part.reshape(MLA_Q_LORA, -1), MLA_QK, HEAD_PAD),
    ], axis=1).astype(BF16)
    wkv = mla_w_kv_b[i].reshape(MLA_KV_LORA, N_HEADS, MLA_NOPE + MLA_V)
    wkv2 = _pad_heads(wkv[..., :MLA_NOPE].reshape(MLA_KV_LORA, -1), MLA_NOPE, HEAD_PAD).astype(BF16)
    wvt = _pad_heads(wkv[..., MLA_NOPE:].reshape(MLA_KV_LORA, -1), MLA_V, VT_ROWS).T.astype(BF16)

    def norm_pair(nvec):
        full = jnp.pad(nvec, (0, LANES - MLA_QK)).reshape(1, LANES)
        rope = nvec[MLA_NOPE:]
        half = MLA_ROPE // 2
        part = jnp.concatenate([rope[half:], rope[:half]])
        part = jnp.pad(part, (MLA_NOPE, LANES - MLA_QK)).reshape(1, LANES)
        return full, part

    bound = MLA_QK * _Q_FOLD * jnp.max(jnp.abs(mla_qk_norm_q[i])) * jnp.max(jnp.abs(mla_qk_norm_k[i]))
    bound = bound.astype(BF16).astype(F32)
    lane = jnp.arange(LANES) == SHIFT_LANE
    qshift = jnp.where(lane, -bound, 0.0).reshape(1, LANES).astype(F32)
    kone = lane.astype(F32).reshape(1, LANES)
    nq, nqp = norm_pair(mla_qk_norm_q[i])
    nk, nkp = norm_pair(mla_qk_norm_k[i])
    w_out = ab_w_out[i].astype(BF16)
    return dict(
        nw=norm_w[l, 1].reshape(1, D_MODEL), w1=w1, q_norm=mla_q_norm[i].reshape(1, -1), wq=wq2,
        kv_norm=mla_kv_norm[i].reshape(1, -1), wkv=wkv2, wvt=wvt, nq=nq, nqp=nqp, nk=nk, nkp=nkp,
        qshift=qshift, kone=kone, logit_bound=bound,
        conv_w=gdn_conv_w[i], a_log=jnp.repeat(gdn_a_log[i], GDN_DK).reshape(1, GDN_W),
        dt_bias=jnp.repeat(gdn_dt_bias[i], GDN_DK).reshape(1, GDN_W),
        onorm2=jnp.tile(gdn_out_norm[i], 2).reshape(1, LANES),
        w_out_a=w_out[:N_HEADS * MLA_V], w_out_b=w_out[N_HEADS * MLA_V:],
    )


def kernel(x, positions, norm_w, ffn_w_gate, ffn_w_up, ffn_w_down, ab_w_in, mla_q_norm, mla_w_q_b, mla_kv_norm, mla_w_kv_b, mla_qk_norm_q, mla_qk_norm_k, gdn_conv_w, gdn_a_log, gdn_dt_bias, gdn_out_norm, ab_w_out, sg_w_in, sg_v_norm, sg_w_s, sg_b_s, sg_w_out):
    batch, seq, _ = x.shape
    depth = norm_w.shape[0]
    t = batch * seq
    xf = x.reshape(t, D_MODEL)
    cos_full, sin_full = _rope_tables(positions)

    wg = ffn_w_gate.astype(BF16)
    wu = ffn_w_up.astype(BF16)
    wd = ffn_w_down.astype(BF16)

    def ffn(xf, l, j, slot, proj=None):
        return _ffn(xf, norm_w[l, slot].reshape(1, D_MODEL), wg, wu, wd, l, j, proj=proj)

    for l in range(depth):
        i = l // 2
        xf = ffn(xf, l, 0, 0)
        proj = None
        if l % 2 == 0:
            wts = _even_weights(l, i, norm_w, ab_w_in, mla_q_norm, mla_w_q_b, mla_kv_norm, mla_w_kv_b,
                                mla_qk_norm_q, mla_qk_norm_k, gdn_conv_w, gdn_a_log, gdn_dt_bias,
                                gdn_out_norm, ab_w_out)
            q, k, vt, qkv, beta, g, z = _ab_in(xf, cos_full, sin_full, wts, seq=seq)
            y_mla = _attention(q, k, vt, wts["logit_bound"], batch=batch, seq=seq)
            y_gdn = _gdn(qkv, beta, g, z, wts["onorm2"], batch=batch, seq=seq)
            proj = (y_mla, y_gdn, wts["w_out_a"], wts["w_out_b"])
        else:
            bs = jnp.broadcast_to(sg_b_s[i][:, :, None], (SG_GROUPS, SG_CHUNK, SG_CG))
            xf = _sg_mixer(xf, norm_w[l, 1].reshape(1, D_MODEL), sg_w_in[i].astype(BF16),
                           sg_v_norm[i].reshape(1, SG_WIDTH), sg_w_s[i], bs, sg_w_out[i].astype(BF16))
        xf = ffn(xf, l, 1, 2, proj=proj)
    return xf.reshape(batch, seq, D_MODEL)
```

```python
import functools

import numpy as np
import jax
import jax.numpy as jnp
from jax import lax
from jax.experimental import pallas as pl
from jax.experimental.pallas import tpu as pltpu

F32 = jnp.float32
BF16 = jnp.bfloat16

D_MODEL = 1024
D_FF = 2816
EPS = 1e-6
N_HEADS = 8
MLA_Q_LORA = 256
MLA_KV_LORA = 128
MLA_NOPE = 64
MLA_ROPE = 32
MLA_V = 64
MLA_QK = MLA_NOPE + MLA_ROPE
ROPE_BASE = 10000.0
GDN_DK = 64
GDN_DV = 64
GDN_CONV = 4
GDN_CHUNK = 64
GDN_W = N_HEADS * GDN_DK
SG_GROUPS = 8
SG_WIDTH = 2048
SG_CHUNK = 128
SG_CG = SG_WIDTH // SG_GROUPS

LANES = 128
MXU_TILE = 256
VMEM_LIMIT_BYTES = 56 * 1024 * 1024
HEAD_PAD = LANES
VT_ROWS = 128
SHIFT_LANE = MLA_QK
FIXED_REF_MAX_BOUND = 60.0
GDN_GROUP = 4
SG_SUB = 256
AB_SUB = 256
_C_Q0 = 0
_C_KV0 = _C_Q0 + MLA_Q_LORA
_C_KR0 = _C_KV0 + MLA_KV_LORA
_C_KRP0 = _C_KR0 + LANES
_C_QKV0 = _C_KRP0 + LANES
_C_Z0 = _C_QKV0 + 3 * GDN_W
_C_B0 = _C_Z0 + GDN_W
_C_A0 = _C_B0 + GDN_W
IN_AB_PAD = _C_A0 + GDN_W

_Q_FOLD = (MLA_QK ** -0.5) * float(np.log2(np.e))


def _cparams(sem):
    return pltpu.CompilerParams(dimension_semantics=sem, vmem_limit_bytes=VMEM_LIMIT_BYTES)


def _rms(x, w):
    return x * lax.rsqrt(jnp.mean(x * x, axis=-1, keepdims=True) + EPS) * w


def _const_spec(shape):
    nd = len(shape)
    return pl.BlockSpec(shape, lambda *_: (0,) * nd, pipeline_mode=pl.Buffered(1))


def _ffn_chunks():
    tiles = D_FF // MXU_TILE
    first = (tiles + 1) // 2 * MXU_TILE
    return ((0, first), (first, D_FF))


def _swiglu_residual(x, nw_ref, wg_ref, wu_ref, wd_ref):
    h = _rms(x, nw_ref[...]).astype(BF16)
    acc = None
    for lo, hi in _ffn_chunks():
        sl = slice(lo, hi)
        g = jnp.dot(h, wg_ref[:, sl], preferred_element_type=F32)
        u = jnp.dot(h, wu_ref[:, sl], preferred_element_type=F32)
        a = (g * jax.nn.sigmoid(g) * u).astype(BF16)
        d = jnp.dot(a, wd_ref[sl, :], preferred_element_type=F32)
        acc = d if acc is None else acc + d
    return x + 0.5 * acc


def _ffn_body(x_ref, nw_ref, wg_ref, wu_ref, wd_ref, o_ref):
    o_ref[...] = _swiglu_residual(x_ref[...], nw_ref, wg_ref, wu_ref, wd_ref)


def _proj_ffn_body(x_ref, ya_ref, yb_ref, wa_ref, wb_ref, nw_ref, wg_ref, wu_ref, wd_ref, o_ref):
    x = x_ref[...] + _dot(ya_ref[...], wa_ref[...]) + _dot(yb_ref[...], wb_ref[...])
    o_ref[...] = _swiglu_residual(x, nw_ref, wg_ref, wu_ref, wd_ref)


def _ffn(x, nw, wg, wu, wd, l, j, *, proj=None, tm=512):
    t = x.shape[0]
    row = lambda i: (i, 0)

    def stacked_spec(w):
        return pl.BlockSpec((None, None) + w.shape[2:], lambda i: (l, j, 0, 0), pipeline_mode=pl.Buffered(1))

    ffn_consts = [nw, wg, wu, wd]
    ffn_specs = [_const_spec(nw.shape), stacked_spec(wg), stacked_spec(wu), stacked_spec(wd)]
    if proj is None:
        body, acts, consts, cspecs = _ffn_body, [x], ffn_consts, ffn_specs
    else:
        ya, yb, wa, wb = proj
        body, acts, consts = _proj_ffn_body, [x, ya, yb], [wa, wb] + ffn_consts
        cspecs = [_const_spec(wa.shape), _const_spec(wb.shape)] + ffn_specs
    return pl.pallas_call(
        body,
        out_shape=jax.ShapeDtypeStruct((t, D_MODEL), F32),
        grid=(t // tm,),
        in_specs=[pl.BlockSpec((tm, a.shape[1]), row) for a in acts] + cspecs,
        out_specs=pl.BlockSpec((tm, D_MODEL), row),
        compiler_params=_cparams(("arbitrary",)),
        name="ffn" if proj is None else "proj_ffn",
    )(*acts, *consts)


def _rope_body(pos_ref, invf_ref, cos_ref, sin_ref):
    ang = pos_ref[...] * invf_ref[...]
    cos_ref[...] = jnp.cos(ang)
    sin_ref[...] = jnp.sin(ang)


def _rope_tables(positions):
    b, s = positions.shape
    t = b * s
    half = MLA_ROPE // 2
    inv_freq = ROPE_BASE ** (-jnp.arange(0, MLA_ROPE, 2, dtype=F32) / MLA_ROPE)
    per_row = LANES // half
    pos = jnp.repeat(positions.reshape(t).astype(F32), half).reshape(t // per_row, LANES)
    invf = jnp.tile(inv_freq, per_row).reshape(1, LANES)
    rows = t // per_row
    tr = min(rows, 512)
    cos, sin = pl.pallas_call(
        _rope_body,
        out_shape=(jax.ShapeDtypeStruct((rows, LANES), F32),) * 2,
        grid=(rows // tr,),
        in_specs=[pl.BlockSpec((tr, LANES), lambda i: (i, 0)), _const_spec((1, LANES))],
        out_specs=(pl.BlockSpec((tr, LANES), lambda i: (i, 0)),) * 2,
        compiler_params=_cparams(("arbitrary",)),
        name="rope_tables",
    )(pos, invf)
    cos = cos.reshape(t, half)
    sin = sin.reshape(t, half)
    pad = LANES - MLA_QK
    cos_full = jnp.concatenate([jnp.ones((t, MLA_NOPE), F32), cos, cos, jnp.ones((t, pad), F32)], axis=-1)
    sin_full = jnp.concatenate([jnp.zeros((t, MLA_NOPE), F32), sin, sin, jnp.zeros((t, pad), F32)], axis=-1)
    return cos_full, sin_full


def _softplus(x):
    return jnp.maximum(x, 0.0) + jnp.log1p(jnp.exp(-jnp.abs(x)))


def _ab_in_body(x_ref, cos_ref, sin_ref, nw_ref, w1_ref, qn_ref, wq_ref, kvn_ref, wkv_ref, wvt_ref,
                nq_ref, nqp_ref, nk_ref, nkp_ref, qshift_ref, kone_ref, convw_ref, alog_ref, dtb_ref,
                q_out, k_out, vt_out, qkv_out, beta_out, g_out, z_out,
                carry_ref, *, tm, tiles_per_seq):
    i = pl.program_id(0)

    @pl.when(i % tiles_per_seq == 0)
    def _():
        carry_ref[0:8, :] = jnp.zeros((8, 3 * GDN_W), F32)

    hw = N_HEADS * HEAD_PAD
    nsub = tm // AB_SUB
    rows = [slice(s * AB_SUB, (s + 1) * AB_SUB) for s in range(nsub)]
    p = [jnp.dot(_rms(x_ref[r, :], nw_ref[...]).astype(BF16), w1_ref[...], preferred_element_type=F32)
         for r in rows]

    qq, kk = [], []
    for s in range(nsub):
        cqn = _rms(p[s][:, _C_Q0:_C_KV0], qn_ref[...]).astype(BF16)
        qq.append(jnp.dot(cqn, wq_ref[...], preferred_element_type=F32))
        ckvn = _rms(p[s][:, _C_KV0:_C_KR0], kvn_ref[...]).astype(BF16)
        kk.append(jnp.dot(ckvn, wkv_ref[...], preferred_element_type=F32))
        vt = lax.dot_general(wvt_ref[...], ckvn, (((1,), (1,)), ((), ())), preferred_element_type=F32)
        vrow = lax.broadcasted_iota(jnp.int32, vt.shape, 0) & (VT_ROWS - 1)
        vt_out[0, :, rows[s]] = jnp.where(vrow == MLA_V, 1.0, vt).astype(BF16)
    for s in range(nsub):
        kr = p[s][:, _C_KR0:_C_KRP0]
        krp = p[s][:, _C_KRP0:_C_QKV0]
        cos = cos_ref[rows[s], :]
        sin = sin_ref[rows[s], :]
        for h in range(N_HEADS):
            sl = slice(h * HEAD_PAD, (h + 1) * HEAD_PAD)
            qh = qq[s][:, sl]
            qph = qq[s][:, hw + h * HEAD_PAD: hw + (h + 1) * HEAD_PAD]
            rq = lax.rsqrt(jnp.sum(qh * qh, axis=-1, keepdims=True) * (1.0 / MLA_QK) + EPS)
            qr = (qh * nq_ref[...] * cos + qph * nqp_ref[...] * sin) * (rq * _Q_FOLD)
            q_out[rows[s], sl] = (qr + qshift_ref[...]).astype(BF16)
            kh = kk[s][:, sl] + kr
            rk = lax.rsqrt(jnp.sum(kh * kh, axis=-1, keepdims=True) * (1.0 / MLA_QK) + EPS)
            kro = (kh * nk_ref[...] * cos + krp * nkp_ref[...] * sin) * rk
            k_out[rows[s], sl] = (kro + kone_ref[...]).astype(BF16)

    for s in range(nsub):
        qkv_pre = p[s][:, _C_QKV0:_C_Z0]
        base = 8 + s * AB_SUB
        carry_ref[base:base + AB_SUB, :] = qkv_pre
        conv = qkv_pre * convw_ref[GDN_CONV - 1:GDN_CONV, :]
        for j in range(GDN_CONV - 1):
            off = base - (GDN_CONV - 1) + j
            conv = conv + carry_ref[off:off + AB_SUB, :] * convw_ref[j:j + 1, :]
        qkv_out[rows[s], :] = conv * jax.nn.sigmoid(conv)
        beta_out[rows[s], :] = jax.nn.sigmoid(p[s][:, _C_B0:_C_A0])
        g_out[rows[s], :] = -jnp.exp(alog_ref[...]) * _softplus(p[s][:, _C_A0:IN_AB_PAD] + dtb_ref[...])
        z_out[rows[s], :] = p[s][:, _C_Z0:_C_B0]
    carry_ref[0:8, :] = carry_ref[tm:tm + 8, :]


def _ab_in(x, cos_full, sin_full, wts, *, seq, tm=512):
    t = x.shape[0]
    tiles_per_seq = seq // tm
    row = lambda i: (i, 0)
    hw = N_HEADS * HEAD_PAD
    outs = (
        jax.ShapeDtypeStruct((t, hw), BF16),
        jax.ShapeDtypeStruct((t, hw), BF16),
        jax.ShapeDtypeStruct((t // tm, N_HEADS * VT_ROWS, tm), BF16),
        jax.ShapeDtypeStruct((t, 3 * GDN_W), F32),
        jax.ShapeDtypeStruct((t, GDN_W), F32),
        jax.ShapeDtypeStruct((t, GDN_W), F32),
        jax.ShapeDtypeStruct((t, GDN_W), F32),
    )
    consts = [wts["nw"], wts["w1"], wts["q_norm"], wts["wq"], wts["kv_norm"], wts["wkv"], wts["wvt"],
              wts["nq"], wts["nqp"], wts["nk"], wts["nkp"], wts["qshift"], wts["kone"], wts["conv_w"], wts["a_log"], wts["dt_bias"]]
    out_specs = tuple(
        pl.BlockSpec((1,) + o.shape[1:], lambda i: (i, 0, 0)) if len(o.shape) == 3
        else pl.BlockSpec((tm, o.shape[1]), row) for o in outs)
    return pl.pallas_call(
        functools.partial(_ab_in_body, tm=tm, tiles_per_seq=tiles_per_seq),
        out_shape=outs,
        grid=(t // tm,),
        in_specs=[pl.BlockSpec((tm, D_MODEL), row), pl.BlockSpec((tm, LANES), row),
                  pl.BlockSpec((tm, LANES), row)] + [_const_spec(c.shape) for c in consts],
        out_specs=out_specs,
        scratch_shapes=[pltpu.VMEM((tm + 8, 3 * GDN_W), F32)],
        compiler_params=_cparams(("arbitrary",)),
        name="ab_in",
    )(x, cos_full, sin_full, *consts)


def _attn_body(q_ref, k_ref, vt_ref, o_ref, m_ref, acc_ref, *, tq, tk, vtile, nh, online):
    qi = pl.program_id(2)
    nt = (((1,), (1,)), ((), ()))
    sub = tk // vtile
    if online:
        m_ref[...] = jnp.full(m_ref.shape, -jnp.inf, F32)
    acc_ref[...] = jnp.zeros(acc_ref.shape, F32)

    def block(kb, masked):
        row0 = pl.multiple_of(kb * tk, tk)
        def scores(e):
            q = q_ref[:, e * HEAD_PAD:(e + 1) * HEAD_PAD]
            k = k_ref[pl.ds(row0, tk), e * HEAD_PAD:(e + 1) * HEAD_PAD]
            s = lax.dot_general(k, q, nt, preferred_element_type=F32)
            if masked:
                key = kb * tk + lax.broadcasted_iota(jnp.int32, (tk, tq), 0)
                qry = qi * tq + lax.broadcasted_iota(jnp.int32, (tk, tq), 1)
                s = jnp.where(key <= qry, s, -jnp.inf)
            return s

        st = [scores(e) for e in range(nh)]
        if online:
            m_new = [jnp.maximum(m_ref[e], jnp.max(st[e], axis=0, keepdims=True)) for e in range(nh)]
            alpha = [jnp.exp2(m_ref[e] - m_new[e]) for e in range(nh)]
            pt = [jnp.exp2(st[e] - m_new[e]).astype(BF16) for e in range(nh)]
        else:
            pt = [jnp.exp2(st[e]).astype(BF16) for e in range(nh)]
        for e in range(nh):
            pv = None
            for j in range(sub):
                vt = vt_ref[kb * sub + j, e * VT_ROWS:(e + 1) * VT_ROWS, :]
                d = _dot(vt, pt[e][j * vtile:(j + 1) * vtile, :])
                pv = d if pv is None else pv + d
            if online:
                m_ref[e] = m_new[e]
                acc_ref[e] = alpha[e] * acc_ref[e] + pv
            else:
                acc_ref[e] = acc_ref[e] + pv

    def body(kb, carry):
        block(kb, False)
        return carry

    nfull = (qi * tq) // tk
    lax.fori_loop(0, nfull, body, 0)
    for j in range(max(tq // tk, 1)):
        block(nfull + j, True)
    for pr in range(nh // 2):
        halves = []
        for e in (2 * pr, 2 * pr + 1):
            acc = acc_ref[e]
            halves.append(acc[:MLA_V] * (1.0 / acc[MLA_V:MLA_V + 1]))
        o_ref[:, pr * 2 * MLA_V:(pr + 1) * 2 * MLA_V] = jnp.concatenate(halves, axis=0).T.astype(o_ref.dtype)


def _attention(q, k, vt, logit_bound, *, batch, seq, tq=512, tk=512, nh=8):
    t = q.shape[0]
    vtile = vt.shape[2]
    assert tk % vtile == 0 and (tq % tk == 0 or tk % tq == 0) and seq % tk == 0 and seq % tq == 0
    nq = seq // tq
    nv = seq // vtile

    def run(online):
        return pl.pallas_call(
            functools.partial(_attn_body, tq=tq, tk=tk, vtile=vtile, nh=nh, online=online),
            out_shape=jax.ShapeDtypeStruct((t, N_HEADS * MLA_V), BF16),
            grid=(batch, N_HEADS // nh, nq),
            in_specs=[
                pl.BlockSpec((tq, nh * HEAD_PAD), lambda b, p, i: (b * nq + i, p)),
                pl.BlockSpec((seq, nh * HEAD_PAD), lambda b, p, i: (b, p), pipeline_mode=pl.Buffered(1)),
                pl.BlockSpec((nv, nh * VT_ROWS, vtile), lambda b, p, i: (b, p, 0), pipeline_mode=pl.Buffered(1)),
            ],
            out_specs=pl.BlockSpec((tq, nh * MLA_V), lambda b, p, i: (b * nq + i, p)),
            scratch_shapes=[pltpu.VMEM((nh, 1, tq), F32), pltpu.VMEM((nh, VT_ROWS, tq), F32)],
            compiler_params=_cparams(("arbitrary", "arbitrary", "arbitrary")),
            name="mla_attention_online" if online else "mla_attention",
        )(q, k, vt)

    return lax.cond(logit_bound <= FIXED_REF_MAX_BOUND, lambda: run(False), lambda: run(True))


def _dot(a, b):
    return jnp.dot(a, b, preferred_element_type=F32)


def _dot_exact_lhs(a_bf16, b):
    hi = b.astype(BF16)
    lo = (b - hi.astype(F32)).astype(BF16)
    return _dot(a_bf16, hi) + _dot(a_bf16, lo)


def _gdn_body(qkv_ref, beta_ref, g_ref, z_ref, onorm_ref, o_ref, state_ref, *, tc):
    c = GDN_CHUNK
    npairs = N_HEADS // 2

    @pl.when(pl.program_id(1) == 0)
    def _():
        state_ref[...] = jnp.zeros(state_ref.shape, F32)

    ri = lax.broadcasted_iota(jnp.int32, (c, LANES), 0)
    li = lax.broadcasted_iota(jnp.int32, (c, LANES), 1)
    lj = li & (c - 1)
    tril2 = lj <= ri
    strict2 = lj < ri
    eye2 = (lj == ri).astype(F32)
    r2 = lax.broadcasted_iota(jnp.int32, (2 * c, LANES), 0)
    l2 = lax.broadcasted_iota(jnp.int32, (2 * c, LANES), 1)
    blkmask = (r2 < c) == (l2 < c)
    gsum_m = blkmask.astype(BF16)
    r4 = lax.broadcasted_iota(jnp.int32, (2 * LANES, 2 * LANES), 0)
    l4 = lax.broadcasted_iota(jnp.int32, (2 * LANES, 2 * LANES), 1)
    gsum2_m = ((r4 >> 6) == (l4 >> 6)).astype(BF16)
    rr = lax.broadcasted_iota(jnp.int32, (c, c), 0)
    cc = lax.broadcasted_iota(jnp.int32, (c, c), 1)
    ltri = (cc <= rr).astype(BF16)
    first_half = li < c

    def blk(rp):
        return jnp.where(blkmask, jnp.concatenate([rp, rp], axis=0), 0.0)

    def mm(lp, rp):
        return _dot(lp.astype(BF16), blk(rp).astype(BF16))


    nt = (((1,), (1,)), ((), ()))
    tn = (((0,), (0,)), ((), ()))
    level_masks = []
    for sh in (3, 4, 5):
        same = (ri >> (sh + 1)) == (lj >> (sh + 1))
        level_masks.append(same & (((ri >> sh) & 1) == 1) & (((lj >> sh) & 1) == 0))
    diag8 = ((ri >> 3) == (lj >> 3)) & strict2

    nch = tc // c
    rows = lambda ch: slice(ch * c, (ch + 1) * c)
    ls = lambda p: slice(p * LANES, (p + 1) * LANES)
    qn_t, kn_t = [], []
    for p in range(npairs):
        q = qkv_ref[:, p * LANES:(p + 1) * LANES]
        k = qkv_ref[:, GDN_W + p * LANES:GDN_W + (p + 1) * LANES]
        ssq = _dot(jnp.concatenate([q * q, k * k], axis=1).astype(BF16), gsum2_m)
        qn_t.append(q * lax.rsqrt(ssq[:, :LANES] + EPS) * (GDN_DK ** -0.5))
        kn_t.append(k * lax.rsqrt(ssq[:, LANES:] + EPS))

    v = {}
    st = [state_ref[p] for p in range(npairs)]
    outs = [[] for _ in range(npairs)]

    def prep_stages(chunks):
        probs = [(ch, p) for ch in chunks for p in range(npairs)]

        def load():
            for ch, p in probs:
                pr = (ch, p)
                bt = beta_ref[rows(ch), ls(p)]
                gg = g_ref[rows(ch), ls(p)]
                v["qn", pr] = qn_t[p][rows(ch), :]
                v["kn", pr] = kn_t[p][rows(ch), :]
                cs = _dot_exact_lhs(ltri, jnp.concatenate([gg, jnp.where(strict2, gg, 0.0)], axis=1))
                v["gcum", pr] = cs[:, :LANES]
                v["decay", pr] = jnp.exp(jnp.where(tril2, cs[:, LANES:], -jnp.inf))
                v["egc", pr] = jnp.exp(v["gcum", pr])
                v["kb", pr] = v["kn", pr] * bt
                v["vb", pr] = qkv_ref[rows(ch), 2 * GDN_W + p * LANES:2 * GDN_W + (p + 1) * LANES] * bt

        def gram():
            for pr in probs:
                lhs = jnp.concatenate([v["kb", pr], v["qn", pr]], axis=0).astype(BF16)
                res = lax.dot_general(lhs, blk(v["kn", pr]).astype(BF16), nt, preferred_element_type=F32)
                v["n", pr] = jnp.where(strict2, res[:c] * v["decay", pr], 0.0)
                v["attn", pr] = res[c:] * v["decay", pr]
                v["nd", pr] = jnp.where(diag8, v["n", pr], 0.0)

        def inv_a():
            for pr in probs:
                v["n2", pr] = mm(v["nd", pr], v["nd", pr])

        def inv_b():
            for pr in probs:
                lhs = jnp.concatenate([eye2 - v["nd", pr], v["n2", pr]], axis=0)
                v["t1n4", pr] = mm(lhs, eye2 + v["n2", pr])

        def inv_c():
            for pr in probs:
                v["tinv", pr] = mm(v["t1n4", pr][:c], eye2 + (v["t1n4", pr][c:] - v["n2", pr]))

        def merge_x(lm):
            def run():
                for pr in probs:
                    v["xk", pr] = mm(v["tinv", pr], jnp.where(lm, v["n", pr], 0.0))
            return run

        def merge_t():
            for pr in probs:
                v["tinv", pr] = v["tinv", pr] - mm(v["xk", pr], v["tinv", pr])

        def solve():
            for pr in probs:
                rhs = jnp.concatenate([blk(v["vb", pr]), blk(v["kb", pr] * v["egc", pr])], axis=1)
                uw = _dot(v["tinv", pr].astype(BF16), rhs.astype(BF16))
                v["u", pr] = uw[:, :LANES]
                glast = v["gcum", pr][c - 1:c, :]
                v["eglast", pr] = jnp.exp(glast)
                v["ktail", pr] = (v["kn", pr] * jnp.exp(glast - v["gcum", pr])).astype(BF16)
                v["wq", pr] = jnp.concatenate([uw[:, LANES:], v["qn", pr] * v["egc", pr]], axis=0).astype(BF16)

        stages = [load, gram, inv_a, inv_b, inv_c]
        for lm in level_masks:
            stages += [merge_x(lm), merge_t]
        return stages + [solve]

    def rec_stages(chunks):
        stages = []
        for ch in chunks:
            def step_a(ch=ch):
                for p in range(npairs):
                    v["ws", p] = _dot(v["wq", (ch, p)], blk(st[p]).astype(BF16))
                    v["vnew", p] = v["u", (ch, p)] - v["ws", p][:c]

            def step_b(ch=ch):
                for p in range(npairs):
                    upd = lax.dot_general(v["ktail", (ch, p)], v["vnew", p].astype(BF16), tn,
                                          preferred_element_type=F32)
                    st[p] = st[p] * v["eglast", (ch, p)] + jnp.where(first_half, upd[0:c, :], upd[c:2 * c, :])
                    outs[p].append(v["ws", p][c:] + mm(v["attn", (ch, p)], v["vnew", p]))

            stages += [step_a, step_b]
        return stages

    groups = [list(range(g0, g0 + GDN_GROUP)) for g0 in range(0, nch, GDN_GROUP)]
    for stage in prep_stages(groups[0]):
        stage()
    for gi in range(1, len(groups)):
        prep, rec = prep_stages(groups[gi]), rec_stages(groups[gi - 1])
        for k in range(max(len(prep), len(rec))):
            if k < len(rec):
                rec[k]()
            if k < len(prep):
                prep[k]()
    for stage in rec_stages(groups[-1]):
        stage()
    for p in range(npairs):
        state_ref[p] = st[p]
        o = jnp.concatenate(outs[p], axis=0)
        ms = _dot((o * o).astype(BF16), gsum_m) * (1.0 / GDN_DV)
        zz = z_ref[:, ls(p)]
        y = o * lax.rsqrt(ms + EPS) * onorm_ref[...] * (zz * jax.nn.sigmoid(zz))
        o_ref[:, ls(p)] = y.astype(o_ref.dtype)


def _gdn(qkv, beta, g, z, onorm2, *, batch, seq, tc=1024):
    t = qkv.shape[0]
    nb = seq // tc
    row = lambda b, i: (b * nb + i, 0)
    return pl.pallas_call(
        functools.partial(_gdn_body, tc=tc),
        out_shape=jax.ShapeDtypeStruct((t, GDN_W), BF16),
        grid=(batch, nb),
        in_specs=[pl.BlockSpec((tc, 3 * GDN_W), row), pl.BlockSpec((tc, GDN_W), row),
                  pl.BlockSpec((tc, GDN_W), row), pl.BlockSpec((tc, GDN_W), row),
                  _const_spec((1, LANES))],
        out_specs=pl.BlockSpec((tc, GDN_W), row),
        scratch_shapes=[pltpu.VMEM((N_HEADS // 2, GDN_DK, LANES), F32)],
        compiler_params=_cparams(("arbitrary", "arbitrary")),
        name="gated_delta_rule",
    )(qkv, beta, g, z, onorm2)


def _sg_body(x_ref, nw_ref, win_ref, vn_ref, ws_ref, bs_ref, wout_ref, o_ref, gate_ref, *, tm):
    rr = lax.broadcasted_iota(jnp.int32, (SG_CHUNK, SG_CHUNK), 0)
    cc = lax.broadcasted_iota(jnp.int32, (SG_CHUNK, SG_CHUNK), 1)
    causal = cc <= rr
    wc = [jnp.where(causal, ws_ref[g], 0.0).astype(BF16) for g in range(SG_GROUPS)]
    nsub = tm // SG_SUB
    rows = [slice(s * SG_SUB, (s + 1) * SG_SUB) for s in range(nsub)]
    uv = []
    for s in range(nsub):
        t = _dot(_rms(x_ref[rows[s], :], nw_ref[...]).astype(BF16), win_ref[...])
        uv.append(0.5 * t * (1.0 + lax.erf(t * (2.0 ** -0.5))))
    for s in range(nsub):
        for g in range(SG_GROUPS):
            cs = slice(SG_WIDTH + g * SG_CG, SG_WIDTH + (g + 1) * SG_CG)
            vg = uv[s][:, cs]
            vg = vg * lax.rsqrt(jnp.mean(vg * vg, axis=-1, keepdims=True) + EPS) * vn_ref[:, g * SG_CG:(g + 1) * SG_CG]
            vg = vg.astype(BF16)
            bias = bs_ref[g]
            for n in range(SG_SUB // SG_CHUNK):
                rs = slice(n * SG_CHUNK, (n + 1) * SG_CHUNK)
                gs = slice(s * SG_SUB + n * SG_CHUNK, s * SG_SUB + (n + 1) * SG_CHUNK)
                gate_ref[gs, g * SG_CG:(g + 1) * SG_CG] = _dot(wc[g], vg[rs, :]) + bias
    for s in range(nsub):
        y = (uv[s][:, :SG_WIDTH] * gate_ref[rows[s], :]).astype(BF16)
        o_ref[rows[s], :] = x_ref[rows[s], :] + _dot(y, wout_ref[...])


def _sg_mixer(x, nw, win, vn, ws, bs, wout, *, tm=512):
    t = x.shape[0]
    row = lambda i: (i, 0)
    consts = [nw, win, vn, ws, bs, wout]
    return pl.pallas_call(
        functools.partial(_sg_body, tm=tm),
        out_shape=jax.ShapeDtypeStruct((t, D_MODEL), F32),
        grid=(t // tm,),
        in_specs=[pl.BlockSpec((tm, D_MODEL), row)] + [_const_spec(c.shape) for c in consts],
        out_specs=pl.BlockSpec((tm, D_MODEL), row),
        scratch_shapes=[pltpu.VMEM((tm, SG_WIDTH), F32)],
        compiler_params=_cparams(("arbitrary",)),
        name="spatial_gating",
    )(x, *consts)


def _pad_heads(w, width, pad_to):
    k = w.shape[0]
    w = w.reshape(k, N_HEADS, width)
    return jnp.pad(w, ((0, 0), (0, 0), (0, pad_to - width))).reshape(k, N_HEADS * pad_to)


def _rot_partner(w_rope):
    half = MLA_ROPE // 2
    return jnp.concatenate([-w_rope[..., half:], w_rope[..., :half]], axis=-1)


def _place_rope(w_rope):
    return jnp.pad(w_rope, ((0, 0), (MLA_NOPE, LANES - MLA_QK)))


def _even_weights(l, i, norm_w, ab_w_in, mla_q_norm, mla_w_q_b, mla_kv_norm, mla_w_kv_b, mla_qk_norm_q,
                  mla_qk_norm_k, gdn_conv_w, gdn_a_log, gdn_dt_bias, gdn_out_norm, ab_w_out):
    w_in = ab_w_in[i]
    o_kr = MLA_Q_LORA + MLA_KV_LORA
    o_qkv = o_kr + MLA_ROPE
    o_z = o_qkv + 3 * GDN_W
    o_b = o_z + GDN_W
    o_a = o_b + N_HEADS
    w_kr = w_in[:, o_kr:o_qkv]
    w1 = jnp.concatenate([
        w_in[:, :o_kr],
        _place_rope(w_kr),
        _place_rope(_rot_partner(w_kr)),
        w_in[:, o_qkv:o_b],
        jnp.repeat(w_in[:, o_b:o_a], GDN_DK, axis=1),
        jnp.repeat(w_in[:, o_a:o_a + N_HEADS], GDN_DK, axis=1),
    ], axis=1).astype(BF16)
    wq = mla_w_q_b[i].reshape(MLA_Q_LORA, N_HEADS, MLA_QK)
    wq_part = jnp.concatenate([jnp.zeros_like(wq[..., :MLA_NOPE]), _rot_partner(wq[..., MLA_NOPE:])], axis=-1)
    wq2 = jnp.concatenate([
        _pad_heads(wq.reshape(MLA_Q_LORA, -1), MLA_QK, HEAD_PAD),
        _pad_heads(wq_part.reshape(MLA_Q_LORA, -1), MLA_QK, HEAD_PAD),
    ], axis=1).astype(BF16)
    wkv = mla_w_kv_b[i].reshape(MLA_KV_LORA, N_HEADS, MLA_NOPE + MLA_V)
    wkv2 = _pad_heads(wkv[..., :MLA_NOPE].reshape(MLA_KV_LORA, -1), MLA_NOPE, HEAD_PAD).astype(BF16)
    wvt = _pad_heads(wkv[..., MLA_NOPE:].reshape(MLA_KV_LORA, -1), MLA_V, VT_ROWS).T.astype(BF16)

    def norm_pair(nvec):
        full = jnp.pad(nvec, (0, LANES - MLA_QK)).reshape(1, LANES)
        rope = nvec[MLA_NOPE:]
        half = MLA_ROPE // 2
        part = jnp.concatenate([rope[half:], rope[:half]])
        part = jnp.pad(part, (MLA_NOPE, LANES - MLA_QK)).reshape(1, LANES)
        return full, part

    bound = MLA_QK * _Q_FOLD * jnp.max(jnp.abs(mla_qk_norm_q[i])) * jnp.max(jnp.abs(mla_qk_norm_k[i]))
    bound = bound.astype(BF16).astype(F32)
    lane = jnp.arange(LANES) == SHIFT_LANE
    qshift = jnp.where(lane, -bound, 0.0).reshape(1, LANES).astype(F32)
    kone = lane.astype(F32).reshape(1, LANES)
    nq, nqp = norm_pair(mla_qk_norm_q[i])
    nk, nkp = norm_pair(mla_qk_norm_k[i])
    w_out = ab_w_out[i].astype(BF16)
    return dict(
        nw=norm_w[l, 1].reshape(1, D_MODEL), w1=w1, q_norm=mla_q_norm[i].reshape(1, -1), wq=wq2,
        kv_norm=mla_kv_norm[i].reshape(1, -1), wkv=wkv2, wvt=wvt, nq=nq, nqp=nqp, nk=nk, nkp=nkp,
        qshift=qshift, kone=kone, logit_bound=bound,
        conv_w=gdn_conv_w[i], a_log=jnp.repeat(gdn_a_log[i], GDN_DK).reshape(1, GDN_W),
        dt_bias=jnp.repeat(gdn_dt_bias[i], GDN_DK).reshape(1, GDN_W),
        onorm2=jnp.tile(gdn_out_norm[i], 2).reshape(1, LANES),
        w_out_a=w_out[:N_HEADS * MLA_V], w_out_b=w_out[N_HEADS * MLA_V:],
    )


def kernel(x, positions, norm_w, ffn_w_gate, ffn_w_up, ffn_w_down, ab_w_in, mla_q_norm, mla_w_q_b, mla_kv_norm, mla_w_kv_b, mla_qk_norm_q, mla_qk_norm_k, gdn_conv_w, gdn_a_log, gdn_dt_bias, gdn_out_norm, ab_w_out, sg_w_in, sg_v_norm, sg_w_s, sg_b_s, sg_w_out):
    batch, seq, _ = x.shape
    depth = norm_w.shape[0]
    t = batch * seq
    xf = x.reshape(t, D_MODEL)
    cos_full, sin_full = _rope_tables(positions)

    wg = ffn_w_gate.astype(BF16)
    wu = ffn_w_up.astype(BF16)
    wd = ffn_w_down.astype(BF16)

    def ffn(xf, l, j, slot, proj=None):
        return _ffn(xf, norm_w[l, slot].reshape(1, D_MODEL), wg, wu, wd, l, j, proj=proj)

    for l in range(depth):
        i = l // 2
        xf = ffn(xf, l, 0, 0)
        proj = None
        if l % 2 == 0:
            wts = _even_weights(l, i, norm_w, ab_w_in, mla_q_norm, mla_w_q_b, mla_kv_norm, mla_w_kv_b,
                                mla_qk_norm_q, mla_qk_norm_k, gdn_conv_w, gdn_a_log, gdn_dt_bias,
                                gdn_out_norm, ab_w_out)
            q, k, vt, qkv, beta, g, z = _ab_in(xf, cos_full, sin_full, wts, seq=seq)
            y_mla = _attention(q, k, vt, wts["logit_bound"], batch=batch, seq=seq)
            y_gdn = _gdn(qkv, beta, g, z, wts["onorm2"], batch=batch, seq=seq)
            proj = (y_mla, y_gdn, wts["w_out_a"], wts["w_out_b"])
        else:
            bs = jnp.broadcast_to(sg_b_s[i][:, :, None], (SG_GROUPS, SG_CHUNK, SG_CG))
            xf = _sg_mixer(xf, norm_w[l, 1].reshape(1, D_MODEL), sg_w_in[i].astype(BF16),
                           sg_v_norm[i].reshape(1, SG_WIDTH), sg_w_s[i], bs, sg_w_out[i].astype(BF16))
        xf = ffn(xf, l, 1, 2, proj=proj)
    return xf.reshape(batch, seq, D_MODEL)
```

```python
import functools

import numpy as np
import jax
import jax.numpy as jnp
from jax import lax
from jax.experimental import pallas as pl
from jax.experimental.pallas import tpu as pltpu

F32 = jnp.float32
BF16 = jnp.bfloat16

D_MODEL = 1024
D_FF = 2816
EPS = 1e-6
N_HEADS = 8
MLA_Q_LORA = 256
MLA_KV_LORA = 128
MLA_NOPE = 64
MLA_ROPE = 32
MLA_V = 64
MLA_QK = MLA_NOPE + MLA_ROPE
ROPE_BASE = 10000.0
GDN_DK = 64
GDN_DV = 64
GDN_CONV = 4
GDN_CHUNK = 64
GDN_W = N_HEADS * GDN_DK
SG_GROUPS = 8
SG_WIDTH = 2048
SG_CHUNK = 128
SG_CG = SG_WIDTH // SG_GROUPS

LANES = 128
MXU_TILE = 256
VMEM_LIMIT_BYTES = 56 * 1024 * 1024
HEAD_PAD = LANES
VT_ROWS = 128
SHIFT_LANE = MLA_QK
FIXED_REF_MAX_BOUND = 60.0
GDN_GROUP = 4
SG_SUB = 256
AB_SUB = 256
_C_Q0 = 0
_C_KV0 = _C_Q0 + MLA_Q_LORA
_C_KR0 = _C_KV0 + MLA_KV_LORA
_C_KRP0 = _C_KR0 + LANES
_C_QKV0 = _C_KRP0 + LANES
_C_Z0 = _C_QKV0 + 3 * GDN_W
_C_B0 = _C_Z0 + GDN_W
_C_A0 = _C_B0 + GDN_W
IN_AB_PAD = _C_A0 + GDN_W

_Q_FOLD = (MLA_QK ** -0.5) * float(np.log2(np.e))


def _cparams(sem):
    return pltpu.CompilerParams(dimension_semantics=sem, vmem_limit_bytes=VMEM_LIMIT_BYTES)


def _rms(x, w):
    return x * lax.rsqrt(jnp.mean(x * x, axis=-1, keepdims=True) + EPS) * w


def _const_spec(shape):
    nd = len(shape)
    return pl.BlockSpec(shape, lambda *_: (0,) * nd, pipeline_mode=pl.Buffered(1))


def _ffn_chunks():
    tiles = D_FF // MXU_TILE
    first = (tiles + 1) // 2 * MXU_TILE
    return ((0, first), (first, D_FF))


def _swiglu_residual(x, nw_ref, wg_ref, wu_ref, wd_ref):
    h = _rms(x, nw_ref[...]).astype(BF16)
    acc = None
    for lo, hi in _ffn_chunks():
        sl = slice(lo, hi)
        g = jnp.dot(h, wg_ref[:, sl], preferred_element_type=F32)
        u = jnp.dot(h, wu_ref[:, sl], preferred_element_type=F32)
        a = (g * jax.nn.sigmoid(g) * u).astype(BF16)
        d = jnp.dot(a, wd_ref[sl, :], preferred_element_type=F32)
        acc = d if acc is None else acc + d
    return x + 0.5 * acc


def _ffn_body(x_ref, nw_ref, wg_ref, wu_ref, wd_ref, o_ref):
    o_ref[...] = _swiglu_residual(x_ref[...], nw_ref, wg_ref, wu_ref, wd_ref)


def _proj_ffn_body(x_ref, ya_ref, yb_ref, wa_ref, wb_ref, nw_ref, wg_ref, wu_ref, wd_ref, o_ref):
    x = x_ref[...] + _dot(ya_ref[...], wa_ref[...]) + _dot(yb_ref[...], wb_ref[...])
    o_ref[...] = _swiglu_residual(x, nw_ref, wg_ref, wu_ref, wd_ref)


def _ffn(x, nw, wg, wu, wd, l, j, *, proj=None, tm=512):
    t = x.shape[0]
    row = lambda i: (i, 0)

    def stacked_spec(w):
        return pl.BlockSpec((None, None) + w.shape[2:], lambda i: (l, j, 0, 0), pipeline_mode=pl.Buffered(1))

    ffn_consts = [nw, wg, wu, wd]
    ffn_specs = [_const_spec(nw.shape), stacked_spec(wg), stacked_spec(wu), stacked_spec(wd)]
    if proj is None:
        body, acts, consts, cspecs = _ffn_body, [x], ffn_consts, ffn_specs
    else:
        ya, yb, wa, wb = proj
        body, acts, consts = _proj_ffn_body, [x, ya, yb], [wa, wb] + ffn_consts
        cspecs = [_const_spec(wa.shape), _const_spec(wb.shape)] + ffn_specs
    return pl.pallas_call(
        body,
        out_shape=jax.ShapeDtypeStruct((t, D_MODEL), F32),
        grid=(t // tm,),
        in_specs=[pl.BlockSpec((tm, a.shape[1]), row) for a in acts] + cspecs,
        out_specs=pl.BlockSpec((tm, D_MODEL), row),
        compiler_params=_cparams(("arbitrary",)),
        name="ffn" if proj is None else "proj_ffn",
    )(*acts, *consts)


def _rope_body(pos_ref, invf_ref, cos_ref, sin_ref):
    ang = pos_ref[...] * invf_ref[...]
    cos_ref[...] = jnp.cos(ang)
    sin_ref[...] = jnp.sin(ang)


def _rope_tables(positions):
    b, s = positions.shape
    t = b * s
    half = MLA_ROPE // 2
    inv_freq = ROPE_BASE ** (-jnp.arange(0, MLA_ROPE, 2, dtype=F32) / MLA_ROPE)
    per_row = LANES // half
    pos = jnp.repeat(positions.reshape(t).astype(F32), half).reshape(t // per_row, LANES)
    invf = jnp.tile(inv_freq, per_row).reshape(1, LANES)
    rows = t // per_row
    tr = min(rows, 512)
    cos, sin = pl.pallas_call(
        _rope_body,
        out_shape=(jax.ShapeDtypeStruct((rows, LANES), F32),) * 2,
        grid=(rows // tr,),
        in_specs=[pl.BlockSpec((tr, LANES), lambda i: (i, 0)), _const_spec((1, LANES))],
        out_specs=(pl.BlockSpec((tr, LANES), lambda i: (i, 0)),) * 2,
        compiler_params=_cparams(("arbitrary",)),
        name="rope_tables",
    )(pos, invf)
    cos = cos.reshape(t, half)
    sin = sin.reshape(t, half)
    pad = LANES - MLA_QK
    cos_full = jnp.concatenate([jnp.ones((t, MLA_NOPE), F32), cos, cos, jnp.ones((t, pad), F32)], axis=-1)
    sin_full = jnp.concatenate([jnp.zeros((t, MLA_NOPE), F32), sin, sin, jnp.zeros((t, pad), F32)], axis=-1)
    return cos_full, sin_full


def _softplus(x):
    return jnp.maximum(x, 0.0) + jnp.log1p(jnp.exp(-jnp.abs(x)))


def _ab_in_body(x_ref, cos_ref, sin_ref, nw_ref, w1_ref, qn_ref, wq_ref, kvn_ref, wkv_ref, wvt_ref,
                nq_ref, nqp_ref, nk_ref, nkp_ref, qshift_ref, kone_ref, convw_ref, alog_ref, dtb_ref,
                q_out, k_out, vt_out, qkv_out, beta_out, g_out, z_out,
                carry_ref, *, tm, tiles_per_seq):
    i = pl.program_id(0)

    @pl.when(i % tiles_per_seq == 0)
    def _():
        carry_ref[0:8, :] = jnp.zeros((8, 3 * GDN_W), F32)

    hw = N_HEADS * HEAD_PAD
    nsub = tm // AB_SUB
    rows = [slice(s * AB_SUB, (s + 1) * AB_SUB) for s in range(nsub)]
    p = [jnp.dot(_rms(x_ref[r, :], nw_ref[...]).astype(BF16), w1_ref[...], preferred_element_type=F32)
         for r in rows]

    qq, kk = [], []
    for s in range(nsub):
        cqn = _rms(p[s][:, _C_Q0:_C_KV0], qn_ref[...]).astype(BF16)
        qq.append(jnp.dot(cqn, wq_ref[...], preferred_element_type=F32))
        ckvn = _rms(p[s][:, _C_KV0:_C_KR0], kvn_ref[...]).astype(BF16)
        kk.append(jnp.dot(ckvn, wkv_ref[...], preferred_element_type=F32))
        vt = lax.dot_general(wvt_ref[...], ckvn, (((1,), (1,)), ((), ())), preferred_element_type=F32)
        vrow = lax.broadcasted_iota(jnp.int32, vt.shape, 0) & (VT_ROWS - 1)
        vt_out[0, :, rows[s]] = jnp.where(vrow == MLA_V, 1.0, vt).astype(BF16)
    for s in range(nsub):
        kr = p[s][:, _C_KR0:_C_KRP0]
        krp = p[s][:, _C_KRP0:_C_QKV0]
        cos = cos_ref[rows[s], :]
        sin = sin_ref[rows[s], :]
        for h in range(N_HEADS):
            sl = slice(h * HEAD_PAD, (h + 1) * HEAD_PAD)
            qh = qq[s][:, sl]
            qph = qq[s][:, hw + h * HEAD_PAD: hw + (h + 1) * HEAD_PAD]
            rq = lax.rsqrt(jnp.sum(qh * qh, axis=-1, keepdims=True) * (1.0 / MLA_QK) + EPS)
            qr = (qh * nq_ref[...] * cos + qph * nqp_ref[...] * sin) * (rq * _Q_FOLD)
            q_out[rows[s], sl] = (qr + qshift_ref[...]).astype(BF16)
            kh = kk[s][:, sl] + kr
            rk = lax.rsqrt(jnp.sum(kh * kh, axis=-1, keepdims=True) * (1.0 / MLA_QK) + EPS)
            kro = (kh * nk_ref[...] * cos + krp * nkp_ref[...] * sin) * rk
            k_out[rows[s], sl] = (kro + kone_ref[...]).astype(BF16)

    for s in range(nsub):
        qkv_pre = p[s][:, _C_QKV0:_C_Z0]
        base = 8 + s * AB_SUB
        carry_ref[base:base + AB_SUB, :] = qkv_pre
        conv = qkv_pre * convw_ref[GDN_CONV - 1:GDN_CONV, :]
        for j in range(GDN_CONV - 1):
            off = base - (GDN_CONV - 1) + j
            conv = conv + carry_ref[off:off + AB_SUB, :] * convw_ref[j:j + 1, :]
        qkv_out[rows[s], :] = conv * jax.nn.sigmoid(conv)
        beta_out[rows[s], :] = jax.nn.sigmoid(p[s][:, _C_B0:_C_A0])
        g_out[rows[s], :] = -jnp.exp(alog_ref[...]) * _softplus(p[s][:, _C_A0:IN_AB_PAD] + dtb_ref[...])
        z_out[rows[s], :] = p[s][:, _C_Z0:_C_B0]
    carry_ref[0:8, :] = carry_ref[tm:tm + 8, :]


def _ab_in(x, cos_full, sin_full, wts, *, seq, tm=512):
    t = x.shape[0]
    tiles_per_seq = seq // tm
    row = lambda i: (i, 0)
    hw = N_HEADS * HEAD_PAD
    outs = (
        jax.ShapeDtypeStruct((t, hw), BF16),
        jax.ShapeDtypeStruct((t, hw), BF16),
        jax.ShapeDtypeStruct((t // tm, N_HEADS * VT_ROWS, tm), BF16),
        jax.ShapeDtypeStruct((t, 3 * GDN_W), F32),
        jax.ShapeDtypeStruct((t, GDN_W), F32),
        jax.ShapeDtypeStruct((t, GDN_W), F32),
        jax.ShapeDtypeStruct((t, GDN_W), F32),
    )
    consts = [wts["nw"], wts["w1"], wts["q_norm"], wts["wq"], wts["kv_norm"], wts["wkv"], wts["wvt"],
              wts["nq"], wts["nqp"], wts["nk"], wts["nkp"], wts["qshift"], wts["kone"], wts["conv_w"], wts["a_log"], wts["dt_bias"]]
    out_specs = tuple(
        pl.BlockSpec((1,) + o.shape[1:], lambda i: (i, 0, 0)) if len(o.shape) == 3
        else pl.BlockSpec((tm, o.shape[1]), row) for o in outs)
    return pl.pallas_call(
        functools.partial(_ab_in_body, tm=tm, tiles_per_seq=tiles_per_seq),
        out_shape=outs,
        grid=(t // tm,),
        in_specs=[pl.BlockSpec((tm, D_MODEL), row), pl.BlockSpec((tm, LANES), row),
                  pl.BlockSpec((tm, LANES), row)] + [_const_spec(c.shape) for c in consts],
        out_specs=out_specs,
        scratch_shapes=[pltpu.VMEM((tm + 8, 3 * GDN_W), F32)],
        compiler_params=_cparams(("arbitrary",)),
        name="ab_in",
    )(x, cos_full, sin_full, *consts)


def _attn_body(q_ref, k_ref, vt_ref, o_ref, m_ref, acc_ref, *, tq, tk, vtile, nh, online):
    qi = pl.program_id(2)
    nt = (((1,), (1,)), ((), ()))
    sub = tk // vtile
    if online:
        m_ref[...] = jnp.full(m_ref.shape, -jnp.inf, F32)
    acc_ref[...] = jnp.zeros(acc_ref.shape, F32)

    def block(kb, masked):
        row0 = pl.multiple_of(kb * tk, tk)
        def scores(e):
            q = q_ref[:, e * HEAD_PAD:(e + 1) * HEAD_PAD]
            k = k_ref[pl.ds(row0, tk), e * HEAD_PAD:(e + 1) * HEAD_PAD]
            s = lax.dot_general(k, q, nt, preferred_element_type=F32)
            if masked:
                key = kb * tk + lax.broadcasted_iota(jnp.int32, (tk, tq), 0)
                qry = qi * tq + lax.broadcasted_iota(jnp.int32, (tk, tq), 1)
                s = jnp.where(key <= qry, s, -jnp.inf)
            return s

        st = [scores(e) for e in range(nh)]
        if online:
            m_new = [jnp.maximum(m_ref[e], jnp.max(st[e], axis=0, keepdims=True)) for e in range(nh)]
            alpha = [jnp.exp2(m_ref[e] - m_new[e]) for e in range(nh)]
            pt = [jnp.exp2(st[e] - m_new[e]).astype(BF16) for e in range(nh)]
        else:
            pt = [jnp.exp2(st[e]).astype(BF16) for e in range(nh)]
        for e in range(nh):
            pv = None
            for j in range(sub):
                vt = vt_ref[kb * sub + j, e * VT_ROWS:(e + 1) * VT_ROWS, :]
                d = _dot(vt, pt[e][j * vtile:(j + 1) * vtile, :])
                pv = d if pv is None else pv + d
            if online:
                m_ref[e] = m_new[e]
                acc_ref[e] = alpha[e] * acc_ref[e] + pv
            else:
                acc_ref[e] = acc_ref[e] + pv

    def body(kb, carry):
        block(kb, False)
        return carry

    nfull = (qi * tq) // tk
    lax.fori_loop(0, nfull, body, 0)
    if online or tq != tk or vtile != tk:
        for j in range(max(tq // tk, 1)):
            block(nfull + j, True)
    else:
        hk = tk // 2
        row0 = pl.multiple_of(nfull * tk, tk)
        tri = (lax.broadcasted_iota(jnp.int32, (hk, hk), 0) <= lax.broadcasted_iota(jnp.int32, (hk, hk), 1))
        pa, pb = [], []
        for e in range(nh):
            q = q_ref[:, e * HEAD_PAD:(e + 1) * HEAD_PAD]
            ka = k_ref[pl.ds(row0, hk), e * HEAD_PAD:(e + 1) * HEAD_PAD]
            kb2 = k_ref[pl.ds(row0 + hk, hk), e * HEAD_PAD:(e + 1) * HEAD_PAD]
            sa = lax.dot_general(ka, q, nt, preferred_element_type=F32)
            sa = jnp.concatenate([jnp.where(tri, sa[:, :hk], -jnp.inf), sa[:, hk:]], axis=1)
            sb = lax.dot_general(kb2, q[hk:, :], nt, preferred_element_type=F32)
            pa.append(jnp.exp2(sa).astype(BF16))
            pb.append(jnp.exp2(jnp.where(tri, sb, -jnp.inf)).astype(BF16))
        for e in range(nh):
            vt = vt_ref[nfull, e * VT_ROWS:(e + 1) * VT_ROWS, :]
            acc_ref[e] = acc_ref[e] + _dot(vt[:, :hk], pa[e])
            acc_ref[e, :, hk:] = acc_ref[e, :, hk:] + _dot(vt[:, hk:], pb[e])
    for pr in range(nh // 2):
        halves = []
        for e in (2 * pr, 2 * pr + 1):
            acc = acc_ref[e]
            halves.append(acc[:MLA_V] * (1.0 / acc[MLA_V:MLA_V + 1]))
        o_ref[:, pr * 2 * MLA_V:(pr + 1) * 2 * MLA_V] = jnp.concatenate(halves, axis=0).T.astype(o_ref.dtype)


def _attention(q, k, vt, logit_bound, *, batch, seq, tq=512, tk=512, nh=8):
    t = q.shape[0]
    vtile = vt.shape[2]
    assert tk % vtile == 0 and (tq % tk == 0 or tk % tq == 0) and seq % tk == 0 and seq % tq == 0
    nq = seq // tq
    nv = seq // vtile

    def run(online):
        return pl.pallas_call(
            functools.partial(_attn_body, tq=tq, tk=tk, vtile=vtile, nh=nh, online=online),
            out_shape=jax.ShapeDtypeStruct((t, N_HEADS * MLA_V), BF16),
            grid=(batch, N_HEADS // nh, nq),
            in_specs=[
                pl.BlockSpec((tq, nh * HEAD_PAD), lambda b, p, i: (b * nq + i, p)),
                pl.BlockSpec((seq, nh * HEAD_PAD), lambda b, p, i: (b, p), pipeline_mode=pl.Buffered(1)),
                pl.BlockSpec((nv, nh * VT_ROWS, vtile), lambda b, p, i: (b, p, 0), pipeline_mode=pl.Buffered(1)),
            ],
            out_specs=pl.BlockSpec((tq, nh * MLA_V), lambda b, p, i: (b * nq + i, p)),
            scratch_shapes=[pltpu.VMEM((nh, 1, tq), F32), pltpu.VMEM((nh, VT_ROWS, tq), F32)],
            compiler_params=_cparams(("arbitrary", "arbitrary", "arbitrary")),
            name="mla_attention_online" if online else "mla_attention",
        )(q, k, vt)

    return lax.cond(logit_bound <= FIXED_REF_MAX_BOUND, lambda: run(False), lambda: run(True))


def _dot(a, b):
    return jnp.dot(a, b, preferred_element_type=F32)


def _dot_exact_lhs(a_bf16, b):
    hi = b.astype(BF16)
    lo = (b - hi.astype(F32)).astype(BF16)
    return _dot(a_bf16, hi) + _dot(a_bf16, lo)


def _gdn_body(qkv_ref, beta_ref, g_ref, z_ref, onorm_ref, o_ref, state_ref, *, tc):
    c = GDN_CHUNK
    npairs = N_HEADS // 2

    @pl.when(pl.program_id(1) == 0)
    def _():
        state_ref[...] = jnp.zeros(state_ref.shape, F32)

    ri = lax.broadcasted_iota(jnp.int32, (c, LANES), 0)
    li = lax.broadcasted_iota(jnp.int32, (c, LANES), 1)
    lj = li & (c - 1)
    tril2 = lj <= ri
    strict2 = lj < ri
    eye2 = (lj == ri).astype(F32)
    r2 = lax.broadcasted_iota(jnp.int32, (2 * c, LANES), 0)
    l2 = lax.broadcasted_iota(jnp.int32, (2 * c, LANES), 1)
    blkmask = (r2 < c) == (l2 < c)
    gsum_m = blkmask.astype(BF16)
    r4 = lax.broadcasted_iota(jnp.int32, (2 * LANES, 2 * LANES), 0)
    l4 = lax.broadcasted_iota(jnp.int32, (2 * LANES, 2 * LANES), 1)
    gsum2_m = ((r4 >> 6) == (l4 >> 6)).astype(BF16)
    rr = lax.broadcasted_iota(jnp.int32, (c, c), 0)
    cc = lax.broadcasted_iota(jnp.int32, (c, c), 1)
    ltri = (cc <= rr).astype(BF16)
    first_half = li < c

    def blk(rp):
        return jnp.where(blkmask, jnp.concatenate([rp, rp], axis=0), 0.0)

    def mm(lp, rp):
        return _dot(lp.astype(BF16), blk(rp).astype(BF16))


    nt = (((1,), (1,)), ((), ()))
    tn = (((0,), (0,)), ((), ()))
    level_masks = []
    for sh in (3, 4, 5):
        same = (ri >> (sh + 1)) == (lj >> (sh + 1))
        level_masks.append(same & (((ri >> sh) & 1) == 1) & (((lj >> sh) & 1) == 0))
    diag8 = ((ri >> 3) == (lj >> 3)) & strict2

    nch = tc // c
    rows = lambda ch: slice(ch * c, (ch + 1) * c)
    ls = lambda p: slice(p * LANES, (p + 1) * LANES)
    qn_t, kn_t = [], []
    for p in range(npairs):
        q = qkv_ref[:, p * LANES:(p + 1) * LANES]
        k = qkv_ref[:, GDN_W + p * LANES:GDN_W + (p + 1) * LANES]
        ssq = _dot(jnp.concatenate([q * q, k * k], axis=1).astype(BF16), gsum2_m)
        qn_t.append(q * lax.rsqrt(ssq[:, :LANES] + EPS) * (GDN_DK ** -0.5))
        kn_t.append(k * lax.rsqrt(ssq[:, LANES:] + EPS))

    v = {}
    st = [state_ref[p] for p in range(npairs)]
    outs = [[] for _ in range(npairs)]

    def prep_stages(chunks):
        probs = [(ch, p) for ch in chunks for p in range(npairs)]

        def load():
            for ch, p in probs:
                pr = (ch, p)
                bt = beta_ref[rows(ch), ls(p)]
                gg = g_ref[rows(ch), ls(p)]
                v["qn", pr] = qn_t[p][rows(ch), :]
                v["kn", pr] = kn_t[p][rows(ch), :]
                cs = _dot_exact_lhs(ltri, jnp.concatenate([gg, jnp.where(strict2, gg, 0.0)], axis=1))
                v["gcum", pr] = cs[:, :LANES]
                v["decay", pr] = jnp.exp(jnp.where(tril2, cs[:, LANES:], -jnp.inf))
                v["egc", pr] = jnp.exp(v["gcum", pr])
                v["kb", pr] = v["kn", pr] * bt
                v["vb", pr] = qkv_ref[rows(ch), 2 * GDN_W + p * LANES:2 * GDN_W + (p + 1) * LANES] * bt

        def gram():
            for pr in probs:
                lhs = jnp.concatenate([v["kb", pr], v["qn", pr]], axis=0).astype(BF16)
                res = lax.dot_general(lhs, blk(v["kn", pr]).astype(BF16), nt, preferred_element_type=F32)
                v["n", pr] = jnp.where(strict2, res[:c] * v["decay", pr], 0.0)
                v["attn", pr] = res[c:] * v["decay", pr]
                v["nd", pr] = jnp.where(diag8, v["n", pr], 0.0)

        def inv_a():
            for pr in probs:
                v["n2", pr] = mm(v["nd", pr], v["nd", pr])

        def inv_b():
            for pr in probs:
                lhs = jnp.concatenate([eye2 - v["nd", pr], v["n2", pr]], axis=0)
                v["t1n4", pr] = mm(lhs, eye2 + v["n2", pr])

        def inv_c():
            for pr in probs:
                v["tinv", pr] = mm(v["t1n4", pr][:c], eye2 + (v["t1n4", pr][c:] - v["n2", pr]))

        def merge_x(lm):
            def run():
                for pr in probs:
                    v["xk", pr] = mm(v["tinv", pr], jnp.where(lm, v["n", pr], 0.0))
            return run

        def merge_t():
            for pr in probs:
                v["tinv", pr] = v["tinv", pr] - mm(v["xk", pr], v["tinv", pr])

        def solve():
            for pr in probs:
                rhs = jnp.concatenate([blk(v["vb", pr]), blk(v["kb", pr] * v["egc", pr])], axis=1)
                uw = _dot(v["tinv", pr].astype(BF16), rhs.astype(BF16))
                v["u", pr] = uw[:, :LANES]
                glast = v["gcum", pr][c - 1:c, :]
                v["eglast", pr] = jnp.exp(glast)
                v["ktail", pr] = (v["kn", pr] * jnp.exp(glast - v["gcum", pr])).astype(BF16)
                v["wq", pr] = jnp.concatenate([uw[:, LANES:], v["qn", pr] * v["egc", pr]], axis=0).astype(BF16)

        stages = [load, gram, inv_a, inv_b, inv_c]
        for lm in level_masks:
            stages += [merge_x(lm), merge_t]
        return stages + [solve]

    def rec_stages(chunks):
        stages = []
        for ch in chunks:
            def step_a(ch=ch):
                for p in range(npairs):
                    v["ws", p] = _dot(v["wq", (ch, p)], blk(st[p]).astype(BF16))
                    v["vnew", p] = v["u", (ch, p)] - v["ws", p][:c]

            def step_b(ch=ch):
                for p in range(npairs):
                    upd = lax.dot_general(v["ktail", (ch, p)], v["vnew", p].astype(BF16), tn,
                                          preferred_element_type=F32)
                    st[p] = st[p] * v["eglast", (ch, p)] + jnp.where(first_half, upd[0:c, :], upd[c:2 * c, :])
                    outs[p].append(v["ws", p][c:] + mm(v["attn", (ch, p)], v["vnew", p]))

            stages += [step_a, step_b]
        return stages

    groups = [list(range(g0, g0 + GDN_GROUP)) for g0 in range(0, nch, GDN_GROUP)]
    for stage in prep_stages(groups[0]):
        stage()
    for gi in range(1, len(groups)):
        prep, rec = prep_stages(groups[gi]), rec_stages(groups[gi - 1])
        for k in range(max(len(prep), len(rec))):
            if k < len(rec):
                rec[k]()
            if k < len(prep):
                prep[k]()
    for stage in rec_stages(groups[-1]):
        stage()
    for p in range(npairs):
        state_ref[p] = st[p]
        o = jnp.concatenate(outs[p], axis=0)
        ms = _dot((o * o).astype(BF16), gsum_m) * (1.0 / GDN_DV)
        zz = z_ref[:, ls(p)]
        y = o * lax.rsqrt(ms + EPS) * onorm_ref[...] * (zz * jax.nn.sigmoid(zz))
        o_ref[:, ls(p)] = y.astype(o_ref.dtype)


def _gdn(qkv, beta, g, z, onorm2, *, batch, seq, tc=1024):
    t = qkv.shape[0]
    nb = seq // tc
    row = lambda b, i: (b * nb + i, 0)
    return pl.pallas_call(
        functools.partial(_gdn_body, tc=tc),
        out_shape=jax.ShapeDtypeStruct((t, GDN_W), BF16),
        grid=(batch, nb),
        in_specs=[pl.BlockSpec((tc, 3 * GDN_W), row), pl.BlockSpec((tc, GDN_W), row),
                  pl.BlockSpec((tc, GDN_W), row), pl.BlockSpec((tc, GDN_W), row),
                  _const_spec((1, LANES))],
        out_specs=pl.BlockSpec((tc, GDN_W), row),
        scratch_shapes=[pltpu.VMEM((N_HEADS // 2, GDN_DK, LANES), F32)],
        compiler_params=_cparams(("arbitrary", "arbitrary")),
        name="gated_delta_rule",
    )(qkv, beta, g, z, onorm2)


def _sg_body(x_ref, nw_ref, win_ref, vn_ref, ws_ref, bs_ref, wout_ref, o_ref, gate_ref, *, tm):
    rr = lax.broadcasted_iota(jnp.int32, (SG_CHUNK, SG_CHUNK), 0)
    cc = lax.broadcasted_iota(jnp.int32, (SG_CHUNK, SG_CHUNK), 1)
    causal = cc <= rr
    wc = [jnp.where(causal, ws_ref[g], 0.0).astype(BF16) for g in range(SG_GROUPS)]
    nsub = tm // SG_SUB
    rows = [slice(s * SG_SUB, (s + 1) * SG_SUB) for s in range(nsub)]
    uv = []
    for s in range(nsub):
        t = _dot(_rms(x_ref[rows[s], :], nw_ref[...]).astype(BF16), win_ref[...])
        uv.append(0.5 * t * (1.0 + lax.erf(t * (2.0 ** -0.5))))
    for s in range(nsub):
        for g in range(SG_GROUPS):
            cs = slice(SG_WIDTH + g * SG_CG, SG_WIDTH + (g + 1) * SG_CG)
            vg = uv[s][:, cs]
            vg = vg * lax.rsqrt(jnp.mean(vg * vg, axis=-1, keepdims=True) + EPS) * vn_ref[:, g * SG_CG:(g + 1) * SG_CG]
            vg = vg.astype(BF16)
            bias = bs_ref[g]
            for n in range(SG_SUB // SG_CHUNK):
                rs = slice(n * SG_CHUNK, (n + 1) * SG_CHUNK)
                gs = slice(s * SG_SUB + n * SG_CHUNK, s * SG_SUB + (n + 1) * SG_CHUNK)
                gate_ref[gs, g * SG_CG:(g + 1) * SG_CG] = _dot(wc[g], vg[rs, :]) + bias
    for s in range(nsub):
        y = (uv[s][:, :SG_WIDTH] * gate_ref[rows[s], :]).astype(BF16)
        o_ref[rows[s], :] = x_ref[rows[s], :] + _dot(y, wout_ref[...])


def _sg_mixer(x, nw, win, vn, ws, bs, wout, *, tm=512):
    t = x.shape[0]
    row = lambda i: (i, 0)
    consts = [nw, win, vn, ws, bs, wout]
    return pl.pallas_call(
        functools.partial(_sg_body, tm=tm),
        out_shape=jax.ShapeDtypeStruct((t, D_MODEL), F32),
        grid=(t // tm,),
        in_specs=[pl.BlockSpec((tm, D_MODEL), row)] + [_const_spec(c.shape) for c in consts],
        out_specs=pl.BlockSpec((tm, D_MODEL), row),
        scratch_shapes=[pltpu.VMEM((tm, SG_WIDTH), F32)],
        compiler_params=_cparams(("arbitrary",)),
        name="spatial_gating",
    )(x, *consts)


def _pad_heads(w, width, pad_to):
    k = w.shape[0]
    w = w.reshape(k, N_HEADS, width)
    return jnp.pad(w, ((0, 0), (0, 0), (0, pad_to - width))).reshape(k, N_HEADS * pad_to)


def _rot_partner(w_rope):
    half = MLA_ROPE // 2
    return jnp.concatenate([-w_rope[..., half:], w_rope[..., :half]], axis=-1)


def _place_rope(w_rope):
    return jnp.pad(w_rope, ((0, 0), (MLA_NOPE, LANES - MLA_QK)))


def _even_weights(l, i, norm_w, ab_w_in, mla_q_norm, mla_w_q_b, mla_kv_norm, mla_w_kv_b, mla_qk_norm_q,
                  mla_qk_norm_k, gdn_conv_w, gdn_a_log, gdn_dt_bias, gdn_out_norm, ab_w_out):
    w_in = ab_w_in[i]
    o_kr = MLA_Q_LORA + MLA_KV_LORA
    o_qkv = o_kr + MLA_ROPE
    o_z = o_qkv + 3 * GDN_W
    o_b = o_z + GDN_W
    o_a = o_b + N_HEADS
    w_kr = w_in[:, o_kr:o_qkv]
    w1 = jnp.concatenate([
        w_in[:, :o_kr],
        _place_rope(w_kr),
        _place_rope(_rot_partner(w_kr)),
        w_in[:, o_qkv:o_b],
        jnp.repeat(w_in[:, o_b:o_a], GDN_DK, axis=1),
        jnp.repeat(w_in[:, o_a:o_a + N_HEADS], GDN_DK, axis=1),
    ], axis=1).astype(BF16)
    wq = mla_w_q_b[i].reshape(MLA_Q_LORA, N_HEADS, MLA_QK)
    wq_part = jnp.concatenate([jnp.zeros_like(wq[..., :MLA_NOPE]), _rot_partner(wq[..., MLA_NOPE:])], axis=-1)
    wq2 = jnp.concatenate([
        _pad_heads(wq.reshape(MLA_Q_LORA, -1), MLA_QK, HEAD_PAD),
        _pad_heads(wq_part.reshape(MLA_Q_LORA, -1), MLA_QK, HEAD_PAD),
    ], axis=1).astype(BF16)
    wkv = mla_w_kv_b[i].reshape(MLA_KV_LORA, N_HEADS, MLA_NOPE + MLA_V)
    wkv2 = _pad_heads(wkv[..., :MLA_NOPE].reshape(MLA_KV_LORA, -1), MLA_NOPE, HEAD_PAD).astype(BF16)
    wvt = _pad_heads(wkv[..., MLA_NOPE:].reshape(MLA_KV_LORA, -1), MLA_V, VT_ROWS).T.astype(BF16)

    def norm_pair(nvec):
        full = jnp.pad(nvec, (0, LANES - MLA_QK)).reshape(1, LANES)
        rope = nvec[MLA_NOPE:]
        half = MLA_ROPE // 2
        part = jnp.concatenate([rope[half:], rope[:half]])
        part = jnp.pad(part, (MLA_NOPE, LANES - MLA_QK)).reshape(1, LANES)
        return full, part

    bound = MLA_QK * _Q_FOLD * jnp.max(jnp.abs(mla_qk_norm_q[i])) * jnp.max(jnp.abs(mla_qk_norm_k[i]))
    bound = bound.astype(BF16).astype(F32)
    lane = jnp.arange(LANES) == SHIFT_LANE
    qshift = jnp.where(lane, -bound, 0.0).reshape(1, LANES).astype(F32)
    kone = lane.astype(F32).reshape(1, LANES)
    nq, nqp = norm_pair(mla_qk_norm_q[i])
    nk, nkp = norm_pair(mla_qk_norm_k[i])
    w_out = ab_w_out[i].astype(BF16)
    return dict(
        nw=norm_w[l, 1].reshape(1, D_MODEL), w1=w1, q_norm=mla_q_norm[i].reshape(1, -1), wq=wq2,
        kv_norm=mla_kv_norm[i].reshape(1, -1), wkv=wkv2, wvt=wvt, nq=nq, nqp=nqp, nk=nk, nkp=nkp,
        qshift=qshift, kone=kone, logit_bound=bound,
        conv_w=gdn_conv_w[i], a_log=jnp.repeat(gdn_a_log[i], GDN_DK).reshape(1, GDN_W),
        dt_bias=jnp.repeat(gdn_dt_bias[i], GDN_DK).reshape(1, GDN_W),
        onorm2=jnp.tile(gdn_out_norm[i], 2).reshape(1, LANES),
        w_out_a=w_out[:N_HEADS * MLA_V], w_out_b=w_out[N_HEADS * MLA_V:],
    )


def kernel(x, positions, norm_w, ffn_w_gate, ffn_w_up, ffn_w_down, ab_w_in, mla_q_norm, mla_w_q_b, mla_kv_norm, mla_w_kv_b, mla_qk_norm_q, mla_qk_norm_k, gdn_conv_w, gdn_a_log, gdn_dt_bias, gdn_out_norm, ab_w_out, sg_w_in, sg_v_norm, sg_w_s, sg_b_s, sg_w_out):
    batch, seq, _ = x.shape
    depth = norm_w.shape[0]
    t = batch * seq
    xf = x.reshape(t, D_MODEL)
    cos_full, sin_full = _rope_tables(positions)

    wg = ffn_w_gate.astype(BF16)
    wu = ffn_w_up.astype(BF16)
    wd = ffn_w_down.astype(BF16)

    def ffn(xf, l, j, slot, proj=None):
        return _ffn(xf, norm_w[l, slot].reshape(1, D_MODEL), wg, wu, wd, l, j, proj=proj)

    for l in range(depth):
        i = l // 2
        xf = ffn(xf, l, 0, 0)
        proj = None
        if l % 2 == 0:
            wts = _even_weights(l, i, norm_w, ab_w_in, mla_q_norm, mla_w_q_b, mla_kv_norm, mla_w_kv_b,
                                mla_qk_norm_q, mla_qk_norm_k, gdn_conv_w, gdn_a_log, gdn_dt_bias,
                                gdn_out_norm, ab_w_out)
            q, k, vt, qkv, beta, g, z = _ab_in(xf, cos_full, sin_full, wts, seq=seq)
            y_mla = _attention(q, k, vt, wts["logit_bound"], batch=batch, seq=seq)
            y_gdn = _gdn(qkv, beta, g, z, wts["onorm2"], batch=batch, seq=seq)
            proj = (y_mla, y_gdn, wts["w_out_a"], wts["w_out_b"])
        else:
            bs = jnp.broadcast_to(sg_b_s[i][:, :, None], (SG_GROUPS, SG_CHUNK, SG_CG))
            xf = _sg_mixer(xf, norm_w[l, 1].reshape(1, D_MODEL), sg_w_in[i].astype(BF16),
                           sg_v_norm[i].reshape(1, SG_WIDTH), sg_w_s[i], bs, sg_w_out[i].astype(BF16))
        xf = ffn(xf, l, 1, 2, proj=proj)
    return xf.reshape(batch, seq, D_MODEL)
```
